```python
import math
import jax
import jax.numpy as jnp
from jax import lax
import numpy as np

D_MODEL = 2048
BATCH = 2
SEQ = 4096
DEPTH = 1

GDN_HEADS = 8
GDN_DK = 128
GDN_DV = 128
ML_HEADS = 4
ML_DQK = 128
ML_DV = 256
GDN_QK_W = GDN_HEADS * GDN_DK
GDN_V_W = GDN_HEADS * GDN_DV
ML_QK_W = ML_HEADS * ML_DQK
ML_V_W = ML_HEADS * ML_DV
D_MIX = GDN_V_W + ML_V_W
D_FF = 5632
CONV_WIDTH = 4
CHUNK = 64
NORM_EPS = 1e-6
GATE_SOFTCAP = 15.0
PART_SIZES = (GDN_QK_W, GDN_QK_W, GDN_V_W, GDN_V_W, GDN_HEADS, GDN_HEADS,
              ML_QK_W, ML_QK_W, ML_V_W, ML_V_W, ML_HEADS, ML_HEADS)
D_IN_PROJ = sum(PART_SIZES)
SPLIT_IDX = tuple(int(s) for s in np.cumsum(PART_SIZES)[:-1])

kernel_name = 'hybrid_gdn_mlstm_macaron_layer'


def rms_norm(x, w):
    xf = x.astype(jnp.float32)
    y = xf * lax.rsqrt(jnp.mean(xf * xf, axis=-1, keepdims=True) + NORM_EPS)
    return (y * w.astype(jnp.float32)).astype(x.dtype)


def swiglu_ffn(x, w_gate, w_up, w_down):
    return (jax.nn.silu(x @ w_gate) * (x @ w_up)) @ w_down


def l2_normalize(x):
    return x * lax.rsqrt(jnp.sum(x * x, axis=-1, keepdims=True) + NORM_EPS)


def soft_cap(x):
    return GATE_SOFTCAP * jnp.tanh(x / GATE_SOFTCAP)


def causal_depthwise_conv(x, w):
    return lax.conv_general_dilated(
        x, w[:, None, :], window_strides=(1,), padding=[(CONV_WIDTH - 1, 0)],
        dimension_numbers=('NWC', 'WIO', 'NWC'), feature_group_count=x.shape[-1])


def to_chunks(x):
    b, t = x.shape[0], x.shape[1]
    x = x.reshape((b, t // CHUNK, CHUNK) + x.shape[2:])
    return jnp.moveaxis(x, 3, 1)


def from_chunks(o):
    nc, b, h, c, d = o.shape
    return o.transpose(1, 0, 3, 2, 4).reshape(b, nc * c, h, d)


def chunk_masks():
    idx = jnp.arange(CHUNK)
    return idx[:, None] >= idx[None, :], idx[:, None] > idx[None, :]


def gated_delta_rule(q, k, v, g, beta):
    causal, strict = chunk_masks()
    gc = jnp.cumsum(g, axis=-1)
    decay = jnp.exp(jnp.where(causal, gc[..., :, None] - gc[..., None, :], -jnp.inf))
    kb = k * beta[..., None]
    m = jnp.where(strict, jnp.einsum('bhnid,bhnjd->bhnij', kb, k) * decay, 0.0)
    eye = jnp.eye(CHUNK, dtype=q.dtype)
    t_inv = lax.linalg.triangular_solve(eye + m, jnp.broadcast_to(eye, m.shape),
                                        left_side=True, lower=True, unit_diagonal=True)
    u = jnp.einsum('bhnij,bhnje->bhnie', t_inv, v * beta[..., None])
    w = jnp.einsum('bhnij,bhnjd->bhnid', t_inv, kb * jnp.exp(gc)[..., None])
    attn = jnp.einsum('bhnid,bhnjd->bhnij', q, k) * decay
    q_dec = q * jnp.exp(gc)[..., None]
    k_dec = k * jnp.exp(gc[..., -1:] - gc)[..., None]
    g_end = jnp.exp(gc[..., -1])
    xs = tuple(jnp.moveaxis(a, 2, 0) for a in (q_dec, w, u, attn, k_dec, g_end))

    def step(state, inp):
        q_c, w_c, u_c, a_c, k_c, ge = inp
        v_new = u_c - jnp.einsum('bhcd,bhde->bhce', w_c, state)
        o_c = jnp.einsum('bhcd,bhde->bhce', q_c, state) + jnp.einsum('bhij,bhje->bhie', a_c, v_new)
        state = ge[..., None, None] * state + jnp.einsum('bhcd,bhce->bhde', k_c, v_new)
        return state, o_c

    s0 = jnp.zeros((q.shape[0], q.shape[1], q.shape[-1], v.shape[-1]), q.dtype)
    _, o = lax.scan(step, s0, xs)
    return o


def mlstm_chunkwise(q, k, v, i_pre, log_f):
    causal, _ = chunk_masks()
    b = jnp.cumsum(log_f, axis=-1)
    dmat = jnp.where(causal, b[..., :, None] - b[..., None, :] + i_pre[..., None, :], -jnp.inf)
    m_intra = jnp.max(dmat, axis=-1)
    qk = jnp.einsum('bhnid,bhnjd->bhnij', q, k)
    w_log = dmat[..., -1, :]
    b_end = b[..., -1]
    xs = tuple(jnp.moveaxis(a, 2, 0) for a in (q, k, v, b, dmat, m_intra, qk, w_log, b_end))

    def step(carry, inp):
        c_st, n_st, m_st = carry
        q_c, k_c, v_c, b_c, d_c, mi_c, qk_c, wl_c, be_c = inp
        inter_log = b_c + m_st[..., None]
        m_t = jnp.maximum(inter_log, mi_c)
        inter = jnp.exp(inter_log - m_t)
        p = jnp.exp(d_c - m_t[..., None]) * qk_c
        num = inter[..., None] * jnp.einsum('bhcd,bhde->bhce', q_c, c_st) + jnp.einsum('bhij,bhje->bhie', p, v_c)
        den = inter * jnp.einsum('bhcd,bhd->bhc', q_c, n_st) + jnp.sum(p, axis=-1)
        h = num / jnp.maximum(jnp.abs(den), jnp.exp(-m_t))[..., None]
        m_new = m_t[..., -1]
        carry_decay = jnp.exp(be_c + m_st - m_new)
        kw = k_c * jnp.exp(wl_c - m_new[..., None])[..., None]
        c_new = carry_decay[..., None, None] * c_st + jnp.einsum('bhcd,bhce->bhde', kw, v_c)
        n_new = carry_decay[..., None] * n_st + jnp.sum(kw, axis=-2)
        return (c_new, n_new, m_new), h

    bsz, heads = q.shape[0], q.shape[1]
    init = (jnp.zeros((bsz, heads, q.shape[-1], v.shape[-1]), q.dtype),
            jnp.zeros((bsz, heads, q.shape[-1]), q.dtype),
            jnp.zeros((bsz, heads), q.dtype))
    _, h = lax.scan(step, init, xs)
    return h


def hybrid_mixer(h, w_in, conv_w, gdn_a_log, gdn_dt_bias, gdn_norm_w, ml_i_bias, ml_f_bias, w_out):
    bsz, t, _ = h.shape
    f32 = jnp.float32
    proj = (h @ w_in).astype(f32)
    gq, gk, gv, gz, ga, gb, mq, mk, mv, mo, mi, mf = jnp.split(proj, SPLIT_IDX, axis=-1)

    qkv = jax.nn.silu(causal_depthwise_conv(jnp.concatenate([gq, gk, gv], axis=-1), conv_w.astype(f32)))
    gq, gk, gv = jnp.split(qkv, [GDN_QK_W, 2 * GDN_QK_W], axis=-1)
    q = l2_normalize(gq.reshape(bsz, t, GDN_HEADS, GDN_DK)) * (GDN_DK ** -0.5)
    k = l2_normalize(gk.reshape(bsz, t, GDN_HEADS, GDN_DK))
    v = gv.reshape(bsz, t, GDN_HEADS, GDN_DV)
    g = -jnp.exp(gdn_a_log.astype(f32)) * jax.nn.softplus(ga + gdn_dt_bias.astype(f32))
    beta = jax.nn.sigmoid(gb)
    o = from_chunks(gated_delta_rule(to_chunks(q), to_chunks(k), to_chunks(v), to_chunks(g), to_chunks(beta)))
    o = o * lax.rsqrt(jnp.mean(o * o, axis=-1, keepdims=True) + NORM_EPS) * gdn_norm_w.astype(f32)
    o = o * jax.nn.silu(gz.reshape(bsz, t, GDN_HEADS, GDN_DV))
    y_gdn = o.reshape(bsz, t, GDN_V_W)

    q = mq.reshape(bsz, t, ML_HEADS, ML_DQK) * (ML_DQK ** -0.5)
    k = mk.reshape(bsz, t, ML_HEADS, ML_DQK)
    v = mv.reshape(bsz, t, ML_HEADS, ML_DV)
    i_pre = soft_cap(mi + ml_i_bias.astype(f32))
    log_f = jax.nn.log_sigmoid(soft_cap(mf + ml_f_bias.astype(f32)))
    hm = from_chunks(mlstm_chunkwise(to_chunks(q), to_chunks(k), to_chunks(v), to_chunks(i_pre), to_chunks(log_f)))
    y_ml = jax.nn.sigmoid(mo) * hm.reshape(bsz, t, ML_V_W)

    y = jnp.concatenate([y_gdn, y_ml], axis=-1).astype(h.dtype)
    return y @ w_out


def setup_inputs(seed: int = 0) -> dict:
    key = jax.random.key(seed)
    ks = jax.random.split(key, 24)
    f32 = jnp.float32

    def normal(k, shape, scale):
        return jax.random.normal(k, shape, f32) * scale

    def gain(k, shape):
        return 1.0 + 0.02 * jax.random.normal(k, shape, f32)

    x = normal(ks[0], (BATCH, SEQ, D_MODEL), 1.0)
    ffn1_norm_w = gain(ks[1], (DEPTH, D_MODEL))
    ffn1_w_gate = normal(ks[2], (DEPTH, D_MODEL, D_FF), D_MODEL ** -0.5)
    ffn1_w_up = normal(ks[3], (DEPTH, D_MODEL, D_FF), D_MODEL ** -0.5)
    ffn1_w_down = normal(ks[4], (DEPTH, D_FF, D_MODEL), D_FF ** -0.5)
    mix_norm_w = gain(ks[5], (DEPTH, D_MODEL))
    w_in = normal(ks[6], (DEPTH, D_MODEL, D_IN_PROJ), D_MODEL ** -0.5)
    conv_w = normal(ks[7], (DEPTH, CONV_WIDTH, 2 * GDN_QK_W + GDN_V_W), CONV_WIDTH ** -0.5)
    gdn_a_log = jnp.log(jax.random.uniform(ks[8], (DEPTH, GDN_HEADS), f32, 1.0, 16.0))
    dt = jnp.exp(jax.random.uniform(ks[9], (DEPTH, GDN_HEADS), f32, math.log(1e-3), math.log(1e-1)))
    gdn_dt_bias = dt + jnp.log(-jnp.expm1(-dt))
    gdn_norm_w = gain(ks[10], (DEPTH, GDN_DV))
    ml_i_bias = normal(ks[11], (DEPTH, ML_HEADS), 0.1)
    ml_f_bias = jnp.linspace(3.0, 6.0, ML_HEADS, dtype=f32)[None, :] + normal(ks[12], (DEPTH, ML_HEADS), 0.1)
    w_out = normal(ks[13], (DEPTH, D_MIX, D_MODEL), D_MIX ** -0.5)
    ffn2_norm_w = gain(ks[14], (DEPTH, D_MODEL))
    ffn2_w_gate = normal(ks[15], (DEPTH, D_MODEL, D_FF), D_MODEL ** -0.5)
    ffn2_w_up = normal(ks[16], (DEPTH, D_MODEL, D_FF), D_MODEL ** -0.5)
    ffn2_w_down = normal(ks[17], (DEPTH, D_FF, D_MODEL), D_FF ** -0.5)
    final_norm_w = gain(ks[18], (D_MODEL,))
    return {'x': x, 'ffn1_norm_w': ffn1_norm_w, 'ffn1_w_gate': ffn1_w_gate, 'ffn1_w_up': ffn1_w_up,
            'ffn1_w_down': ffn1_w_down, 'mix_norm_w': mix_norm_w, 'w_in': w_in, 'conv_w': conv_w,
            'gdn_a_log': gdn_a_log, 'gdn_dt_bias': gdn_dt_bias, 'gdn_norm_w': gdn_norm_w,
            'ml_i_bias': ml_i_bias, 'ml_f_bias': ml_f_bias, 'w_out': w_out, 'ffn2_norm_w': ffn2_norm_w,
            'ffn2_w_gate': ffn2_w_gate, 'ffn2_w_up': ffn2_w_up, 'ffn2_w_down': ffn2_w_down,
            'final_norm_w': final_norm_w}


def reference(x, ffn1_norm_w, ffn1_w_gate, ffn1_w_up, ffn1_w_down, mix_norm_w, w_in, conv_w,
              gdn_a_log, gdn_dt_bias, gdn_norm_w, ml_i_bias, ml_f_bias, w_out, ffn2_norm_w,
              ffn2_w_gate, ffn2_w_up, ffn2_w_down, final_norm_w):
    h = x
    for l in range(DEPTH):
        h = h + 0.5 * swiglu_ffn(rms_norm(h, ffn1_norm_w[l]), ffn1_w_gate[l], ffn1_w_up[l], ffn1_w_down[l])
        h = h + hybrid_mixer(rms_norm(h, mix_norm_w[l]), w_in[l], conv_w[l], gdn_a_log[l], gdn_dt_bias[l],
                             gdn_norm_w[l], ml_i_bias[l], ml_f_bias[l], w_out[l])
        h = h + 0.5 * swiglu_ffn(rms_norm(h, ffn2_norm_w[l]), ffn2_w_gate[l], ffn2_w_up[l], ffn2_w_down[l])
    return rms_norm(h, final_norm_w)
```

```python
import functools

import jax
import jax.numpy as jnp
from jax import lax
from jax.experimental import pallas as pl
from jax.experimental.pallas import tpu as pltpu

F32 = jnp.float32
BF16 = jnp.bfloat16

D_MODEL = 2048
D_FF = 5632
GDN_HEADS = 8
GDN_DK = 128
GDN_DV = 128
ML_HEADS = 4
ML_DQK = 128
ML_DV = 256
GDN_QK_W = GDN_HEADS * GDN_DK
GDN_V_W = GDN_HEADS * GDN_DV
ML_QK_W = ML_HEADS * ML_DQK
ML_V_W = ML_HEADS * ML_DV
CONV_WIDTH = 4
CHUNK = 64
NORM_EPS = 1e-6
GATE_SOFTCAP = 15.0

LANES = 128
SUBLANES = 8
GATE_W = LANES
GA0, GB0, MI0, MF0 = 0, GDN_HEADS, 2 * GDN_HEADS, 2 * GDN_HEADS + ML_HEADS
C_GQKV = 0
C_GZ = 2 * GDN_QK_W + GDN_V_W
C_MQ = C_GZ + GDN_V_W
C_MK = C_MQ + ML_QK_W
C_MV = C_MK + ML_QK_W
C_MO = C_MV + ML_V_W
C_GATE = C_MO + ML_V_W
D_PROJ = C_GATE + GATE_W

TB = 2 * CHUNK
NCB = TB // CHUNK
NEG = -1e30
VMEM_LIMIT = 56 * 1024 * 1024


def _rms(x, w):
    return x * lax.rsqrt(jnp.mean(x * x, axis=-1, keepdims=True) + NORM_EPS) * w


def _softplus(x):
    return jnp.maximum(x, 0.0) + jnp.log1p(jnp.exp(-jnp.abs(x)))


def _dot(a, b):
    return jnp.dot(a, b, preferred_element_type=F32)


def _dot_nt(a, b):
    return lax.dot_general(a, b, (((1,), (1,)), ((), ())), preferred_element_type=F32)


def _dot_tn(a, b):
    return lax.dot_general(a, b, (((0,), (0,)), ((), ())), preferred_element_type=F32)


def _dot_hi(a, b):
    return jnp.dot(a, b, preferred_element_type=F32, precision=lax.Precision.HIGHEST)


def _ffn_kernel(x_ref, nw_ref, wg_ref, wu_ref, wd_ref, fw_ref, o_ref, xn_ref, *, nf, final_norm):
    f = pl.program_id(1)

    @pl.when(f == 0)
    def _():
        xn_ref[...] = _rms(x_ref[...], nw_ref[...]).astype(BF16)

    xn = xn_ref[...]
    g = _dot(xn, wg_ref[...])
    u = _dot(xn, wu_ref[...])
    a = (g * jax.nn.sigmoid(g) * u).astype(BF16)
    contrib = _dot(a, wd_ref[...])

    @pl.when(f == 0)
    def _():
        o_ref[...] = contrib

    @pl.when(f > 0)
    def _():
        o_ref[...] += contrib

    @pl.when(f == nf - 1)
    def _():
        y = x_ref[...] + 0.5 * o_ref[...]
        if final_norm:
            y = _rms(y, fw_ref[...])
        o_ref[...] = y


def _ffn(x, nw, wg, wu, wd, fw, *, final_norm, tm=512, tf=512):
    m, d = x.shape
    dff = wg.shape[1]
    nf = dff // tf
    return pl.pallas_call(
        functools.partial(_ffn_kernel, nf=nf, final_norm=final_norm),
        grid=(m // tm, nf),
        in_specs=[
            pl.BlockSpec((tm, d), lambda i, f: (i, 0)),
            pl.BlockSpec((1, d), lambda i, f: (0, 0)),
            pl.BlockSpec((d, tf), lambda i, f: (0, f)),
            pl.BlockSpec((d, tf), lambda i, f: (0, f)),
            pl.BlockSpec((tf, d), lambda i, f: (f, 0)),
            pl.BlockSpec((1, d), lambda i, f: (0, 0)),
        ],
        out_specs=pl.BlockSpec((tm, d), lambda i, f: (i, 0)),
        out_shape=jax.ShapeDtypeStruct((m, d), F32),
        scratch_shapes=[pltpu.VMEM((tm, d), BF16)],
        compiler_params=pltpu.CompilerParams(
            dimension_semantics=("parallel", "arbitrary"), vmem_limit_bytes=VMEM_LIMIT),
        name="ffn",
    )(x, nw, wg, wu, wd, fw)


def _in_proj_kernel(x_ref, nw_ref, w_ref, wgt_ref, p_ref, gt_ref, xn_ref):
    j = pl.program_id(1)

    @pl.when(j == 0)
    def _():
        xn = _rms(x_ref[...], nw_ref[...]).astype(BF16)
        xn_ref[...] = xn
        gt_ref[...] = _dot_nt(wgt_ref[...], xn)

    p_ref[...] = _dot(xn_ref[...], w_ref[...])


def _in_proj(x, nw, w, wgt, *, tm=512, tn=2432):
    m, d = x.shape
    n = w.shape[1]
    return pl.pallas_call(
        _in_proj_kernel,
        grid=(m // tm, n // tn),
        in_specs=[
            pl.BlockSpec((tm, d), lambda i, j: (i, 0)),
            pl.BlockSpec((1, d), lambda i, j: (0, 0)),
            pl.BlockSpec((d, tn), lambda i, j: (0, j)),
            pl.BlockSpec((GATE_W, d), lambda i, j: (0, 0)),
        ],
        out_specs=[
            pl.BlockSpec((tm, tn), lambda i, j: (i, j)),
            pl.BlockSpec((GATE_W, tm), lambda i, j: (0, i)),
        ],
        out_shape=[jax.ShapeDtypeStruct((m, n), F32), jax.ShapeDtypeStruct((GATE_W, m), F32)],
        scratch_shapes=[pltpu.VMEM((tm, d), BF16)],
        compiler_params=pltpu.CompilerParams(
            dimension_semantics=("parallel", "arbitrary"), vmem_limit_bytes=VMEM_LIMIT),
        name="in_proj",
    )(x, nw, w, wgt)


def _chunk_iotas():
    row = lax.broadcasted_iota(jnp.int32, (CHUNK, CHUNK), 0)
    col = lax.broadcasted_iota(jnp.int32, (CHUNK, CHUNK), 1)
    return row, col


def _cumsum_both(col_vals, row_vals, row, col):
    tril = (row >= col).astype(F32)
    triu = (row <= col).astype(F32)
    return _dot_hi(tril, col_vals), _dot_hi(row_vals, triu)


def _rows(x, c):
    return x[c * CHUNK:(c + 1) * CHUNK]


def _unit_lower_inverse(m, row, col):
    strict = row > col
    eye = (row == col).astype(F32)
    first = strict & (jnp.right_shift(row, 1) == jnp.right_shift(col, 1))
    t_inv = [eye - jnp.where(first, mi, 0.0) for mi in m]
    s = 1
    while (2 << s) <= CHUNK:
        join = (strict & (jnp.right_shift(row, s + 1) == jnp.right_shift(col, s + 1))
                & (jnp.right_shift(row, s) != jnp.right_shift(col, s)))
        cb = [jnp.where(join, mi, 0.0).astype(BF16) for mi in m]
        tb = [ti.astype(BF16) for ti in t_inv]
        x = [_dot(ci, ti).astype(BF16) for ci, ti in zip(cb, tb)]
        t_inv = [ti - _dot(tbi, xi) for ti, tbi, xi in zip(t_inv, tb, x)]
        s += 1
    return t_inv


def _gdn_kernel(qkv_ref, z_ref, gc_ref, gr_ref, cw_ref, prow_ref, pcol_ref, nw_ref,
                y_ref, ext_ref, s_ref):
    t = pl.program_id(1)

    @pl.when(t == 0)
    def _():
        ext_ref[0:SUBLANES, :] = jnp.zeros((SUBLANES, ext_ref.shape[1]), F32)
        s_ref[...] = jnp.zeros(s_ref.shape, F32)

    ext_ref[SUBLANES:SUBLANES + TB, :] = qkv_ref[...]

    def conv_silu(c0):
        acc = None
        for i in range(CONV_WIDTH):
            lo = SUBLANES - (CONV_WIDTH - 1) + i
            term = ext_ref[lo:lo + TB, c0:c0 + LANES] * cw_ref[i:i + 1, c0:c0 + LANES]
            acc = term if acc is None else acc + term
        return acc * jax.nn.sigmoid(acc)

    def l2n(x):
        return x * lax.rsqrt(jnp.sum(x * x, axis=-1, keepdims=True) + NORM_EPS)

    row, col = _chunk_iotas()
    causal = row >= col
    strict = row > col

    gcol = gc_ref[...]
    grow = gr_ref[0:2 * SUBLANES, :]
    g_col = -jnp.exp(prow_ref[1:2, :]) * _softplus(gcol + prow_ref[0:1, :])
    beta_col = jax.nn.sigmoid(gcol)
    g_row = -jnp.exp(pcol_ref[0:2 * SUBLANES, 1:2]) * _softplus(grow + pcol_ref[0:2 * SUBLANES, 0:1])

    heads = range(GDN_HEADS)
    probs = [(c, h) for c in range(NCB) for h in heads]
    q_all = [l2n(conv_silu(h * GDN_DK)) * (GDN_DK ** -0.5) for h in heads]
    k_all = [l2n(conv_silu(GDN_QK_W + h * GDN_DK)) for h in heads]
    v_all = [conv_silu(2 * GDN_QK_W + h * GDN_DV) for h in heads]
    cs = [_cumsum_both(_rows(g_col, c), g_row[0:SUBLANES, c * CHUNK:(c + 1) * CHUNK], row, col)
          for c in range(NCB)]

    q = [_rows(q_all[h], c) for c, h in probs]
    k = [_rows(k_all[h], c) for c, h in probs]
    v = [_rows(v_all[h], c) for c, h in probs]
    gc_c = [cs[c][0][:, GA0 + h:GA0 + h + 1] for c, h in probs]
    gc_r = [cs[c][1][GA0 + h:GA0 + h + 1, :] for c, h in probs]
    beta = [_rows(beta_col, c)[:, GB0 + h:GB0 + h + 1] for c, h in probs]
    decay = [jnp.exp(jnp.where(causal, a - b, NEG)) for a, b in zip(gc_c, gc_r)]
    kb = [ki * bi for ki, bi in zip(k, beta)]
    kq = [_dot_nt(jnp.concatenate([kbi, qi], axis=0).astype(BF16), ki.astype(BF16))
          for kbi, qi, ki in zip(kb, q, k)]
    m = [jnp.where(strict, x[:CHUNK] * d, 0.0) for x, d in zip(kq, decay)]
    attn = [(x[CHUNK:] * d).astype(BF16) for x, d in zip(kq, decay)]
    t_inv = _unit_lower_inverse(m, row, col)

    egc = [jnp.exp(a) for a in gc_c]
    uw = [_dot(ti.astype(BF16), jnp.concatenate([vi * bi, kbi * ei], axis=1).astype(BF16))
          for ti, vi, bi, kbi, ei in zip(t_inv, v, beta, kb, egc)]
    gc_last = [a[CHUNK - 1:CHUNK, :] for a in gc_c]
    k_dec = [(ki * jnp.exp(gl - a)).astype(BF16) for ki, gl, a in zip(k, gc_last, gc_c)]
    g_end = [jnp.exp(gl) for gl in gc_last]
    wq_lhs = [jnp.concatenate([uwi[:, GDN_DV:], qi * ei], axis=0).astype(BF16)
              for uwi, qi, ei in zip(uw, q, egc)]

    st = [s_ref[h] for h in heads]
    o = []
    for c in range(NCB):
        i0 = c * GDN_HEADS
        wq = [_dot(wq_lhs[i0 + h], st[h].astype(BF16)) for h in heads]
        vb = [(uw[i0 + h][:, :GDN_DV] - wq[h][:CHUNK]).astype(BF16) for h in heads]
        o += [wq[h][CHUNK:] + _dot(attn[i0 + h], vb[h]) for h in heads]
        st = [g_end[i0 + h] * st[h] + _dot_tn(k_dec[i0 + h], vb[h]) for h in heads]
    for h in heads:
        s_ref[h] = st[h]

    nw = nw_ref[...]
    for (c, h), oi in zip(probs, o):
        oi = oi * lax.rsqrt(jnp.mean(oi * oi, axis=-1, keepdims=True) + NORM_EPS) * nw
        z = z_ref[c * CHUNK:(c + 1) * CHUNK, h * GDN_DV:(h + 1) * GDN_DV]
        oi = oi * (z * jax.nn.sigmoid(z))
        y_ref[c * CHUNK:(c + 1) * CHUNK, h * GDN_DV:(h + 1) * GDN_DV] = oi.astype(y_ref.dtype)

    ext_ref[0:SUBLANES, :] = ext_ref[TB:TB + SUBLANES, :]


def _gdn(proj, gt, conv_w, prow, pcol, norm_w, *, batch, seq):
    nt = seq // TB
    qkv_w = 2 * GDN_QK_W + GDN_V_W
    return pl.pallas_call(
        _gdn_kernel,
        grid=(batch, nt),
        in_specs=[
            pl.BlockSpec((TB, qkv_w), lambda b, t: (b * nt + t, C_GQKV // qkv_w)),
            pl.BlockSpec((TB, GDN_V_W), lambda b, t: (b * nt + t, C_GZ // GDN_V_W)),
            pl.BlockSpec((TB, GATE_W), lambda b, t: (b * nt + t, C_GATE // GATE_W)),
            pl.BlockSpec((GATE_W, TB), lambda b, t: (0, b * nt + t)),
            pl.BlockSpec((CONV_WIDTH, qkv_w), lambda b, t: (0, 0)),
            pl.BlockSpec((SUBLANES, GATE_W), lambda b, t: (0, 0)),
            pl.BlockSpec((GATE_W, LANES), lambda b, t: (0, 0)),
            pl.BlockSpec((1, GDN_DV), lambda b, t: (0, 0)),
        ],
        out_specs=pl.BlockSpec((TB, GDN_V_W), lambda b, t: (b * nt + t, 0)),
        out_shape=jax.ShapeDtypeStruct((batch * seq, GDN_V_W), BF16),
        scratch_shapes=[
            pltpu.VMEM((TB + SUBLANES, qkv_w), F32),
            pltpu.VMEM((GDN_HEADS, GDN_DK, GDN_DV), F32),
        ],
        compiler_params=pltpu.CompilerParams(
            dimension_semantics=("parallel", "arbitrary"), vmem_limit_bytes=VMEM_LIMIT),
        name="gdn",
    )(proj, proj, proj, gt, conv_w, prow, pcol, norm_w)


def _mlstm_kernel(q_ref, k_ref, v_ref, o_ref, gc_ref, gr_ref, prow_ref, pcol_ref,
                  y_ref, c_ref, n_ref, m_ref):
    t = pl.program_id(1)

    @pl.when(t == 0)
    def _():
        c_ref[...] = jnp.zeros(c_ref.shape, F32)
        n_ref[...] = jnp.zeros(n_ref.shape, F32)
        m_ref[...] = jnp.zeros(m_ref.shape, F32)

    row, col = _chunk_iotas()
    causal = row >= col

    def cap(x):
        return GATE_SOFTCAP * jnp.tanh(x / GATE_SOFTCAP)

    capc = cap(gc_ref[...] + prow_ref[0:1, :])
    logf_col = -_softplus(-capc)
    r_lo = 2 * SUBLANES
    capr = cap(gr_ref[r_lo:r_lo + SUBLANES, :] + pcol_ref[r_lo:r_lo + SUBLANES, 0:1])
    logf_row = -_softplus(-capr)

    heads = range(ML_HEADS)
    probs = [(c, h) for c in range(NCB) for h in heads]
    cs = [_cumsum_both(_rows(logf_col, c), logf_row[:, c * CHUNK:(c + 1) * CHUNK], row, col)
          for c in range(NCB)]
    q = [q_ref[c * CHUNK:(c + 1) * CHUNK, h * ML_DQK:(h + 1) * ML_DQK] * (ML_DQK ** -0.5) for c, h in probs]
    k = [k_ref[c * CHUNK:(c + 1) * CHUNK, h * ML_DQK:(h + 1) * ML_DQK] for c, h in probs]
    qb = [x.astype(BF16) for x in q]
    kb = [x.astype(BF16) for x in k]
    vb = [v_ref[c * CHUNK:(c + 1) * CHUNK, h * ML_DV:(h + 1) * ML_DV].astype(BF16) for c, h in probs]
    b_c = [cs[c][0][:, MF0 + h:MF0 + h + 1] for c, h in probs]
    b_r = [cs[c][1][MF0 - r_lo + h:MF0 - r_lo + h + 1, :] for c, h in probs]
    i_c = [_rows(capc, c)[:, MI0 + h:MI0 + h + 1] for c, h in probs]
    i_r = [capr[MI0 - r_lo + h:MI0 - r_lo + h + 1, c * CHUNK:(c + 1) * CHUNK] for c, h in probs]
    b_end = [x[CHUNK - 1:CHUNK, :] for x in b_c]
    dmat = [jnp.where(causal, bc - br + ir, NEG) for bc, br, ir in zip(b_c, b_r, i_r)]
    m_intra = [jnp.max(d, axis=-1, keepdims=True) for d in dmat]
    qk = [_dot_nt(a, b) for a, b in zip(qb, kb)]

    c_st = [c_ref[h] for h in heads]
    n_st = [n_ref[h:h + 1, :] for h in heads]
    m_st = [m_ref[h:h + 1, 0:1] for h in heads]
    for c in range(NCB):
        ids = [c * ML_HEADS + h for h in heads]
        inter_log = [b_c[i] + m_st[h] for h, i in zip(heads, ids)]
        m_t = [jnp.maximum(a, m_intra[i]) for a, i in zip(inter_log, ids)]
        inter = [jnp.exp(a - b) for a, b in zip(inter_log, m_t)]
        p = [jnp.exp(dmat[i] - mt) * qk[i] for mt, i in zip(m_t, ids)]
        num = [it * _dot(qb[i], c_st[h].astype(BF16)) + _dot(pi.astype(BF16), vb[i])
               for h, i, it, pi in zip(heads, ids, inter, p)]
        den = [it * jnp.sum(q[i] * n_st[h], axis=-1, keepdims=True) + jnp.sum(pi, axis=-1, keepdims=True)
               for h, i, it, pi in zip(heads, ids, inter, p)]
        hout = [nu / jnp.maximum(jnp.abs(de), jnp.exp(-mt)) for nu, de, mt in zip(num, den, m_t)]
        m_new = [mt[CHUNK - 1:CHUNK, :] for mt in m_t]
        carry = [jnp.exp(b_end[i] + m_st[h] - mn) for h, i, mn in zip(heads, ids, m_new)]
        kw = [k[i] * jnp.exp(b_end[i] - b_c[i] + i_c[i] - mn) for i, mn in zip(ids, m_new)]
        c_st = [ca * c_st[h] + _dot_tn(kwi.astype(BF16), vb[i]) for h, i, ca, kwi in zip(heads, ids, carry, kw)]
        n_st = [ca * n_st[h] + jnp.sum(kwi, axis=0, keepdims=True) for h, ca, kwi in zip(heads, carry, kw)]
        m_st = m_new
        for h, ho in zip(heads, hout):
            og = o_ref[c * CHUNK:(c + 1) * CHUNK, h * ML_DV:(h + 1) * ML_DV]
            y_ref[c * CHUNK:(c + 1) * CHUNK, h * ML_DV:(h + 1) * ML_DV] = (
                jax.nn.sigmoid(og) * ho).astype(y_ref.dtype)
    for h in heads:
        c_ref[h] = c_st[h]
        n_ref[h:h + 1, :] = n_st[h]
        m_ref[h:h + 1, :] = jnp.broadcast_to(m_st[h], (1, LANES))


def _mlstm(proj, gt, prow, pcol, *, batch, seq):
    nt = seq // TB
    return pl.pallas_call(
        _mlstm_kernel,
        grid=(batch, nt),
        in_specs=[
            pl.BlockSpec((TB, ML_QK_W), lambda b, t: (b * nt + t, C_MQ // ML_QK_W)),
            pl.BlockSpec((TB, ML_QK_W), lambda b, t: (b * nt + t, C_MK // ML_QK_W)),
            pl.BlockSpec((TB, ML_V_W), lambda b, t: (b * nt + t, C_MV // ML_V_W)),
            pl.BlockSpec((TB, ML_V_W), lambda b, t: (b * nt + t, C_MO // ML_V_W)),
            pl.BlockSpec((TB, GATE_W), lambda b, t: (b * nt + t, C_GATE // GATE_W)),
            pl.BlockSpec((GATE_W, TB), lambda b, t: (0, b * nt + t)),
            pl.BlockSpec((SUBLANES, GATE_W), lambda b, t: (0, 0)),
            pl.BlockSpec((GATE_W, LANES), lambda b, t: (0, 0)),
        ],
        out_specs=pl.BlockSpec((TB, ML_V_W), lambda b, t: (b * nt + t, 0)),
        out_shape=jax.ShapeDtypeStruct((batch * seq, ML_V_W), BF16),
        scratch_shapes=[
            pltpu.VMEM((ML_HEADS, ML_DQK, ML_DV), F32),
            pltpu.VMEM((SUBLANES, ML_DQK), F32),
            pltpu.VMEM((SUBLANES, LANES), F32),
        ],
        compiler_params=pltpu.CompilerParams(
            dimension_semantics=("parallel", "arbitrary"), vmem_limit_bytes=VMEM_LIMIT),
        name="mlstm",
    )(proj, proj, proj, proj, proj, gt, prow, pcol)


def _out_proj_kernel(x_ref, yg_ref, ym_ref, wg_ref, wm_ref, o_ref):
    o_ref[...] = x_ref[...] + _dot(yg_ref[...], wg_ref[...]) + _dot(ym_ref[...], wm_ref[...])


def _out_proj(x, yg, ym, w_g, w_m, *, tm=512):
    m, d = x.shape
    return pl.pallas_call(
        _out_proj_kernel,
        grid=(m // tm,),
        in_specs=[
            pl.BlockSpec((tm, d), lambda i: (i, 0)),
            pl.BlockSpec((tm, GDN_V_W), lambda i: (i, 0)),
            pl.BlockSpec((tm, ML_V_W), lambda i: (i, 0)),
            pl.BlockSpec((GDN_V_W, d), lambda i: (0, 0)),
            pl.BlockSpec((ML_V_W, d), lambda i: (0, 0)),
        ],
        out_specs=pl.BlockSpec((tm, d), lambda i: (i, 0)),
        out_shape=jax.ShapeDtypeStruct((m, d), F32),
        compiler_params=pltpu.CompilerParams(
            dimension_semantics=("parallel",), vmem_limit_bytes=VMEM_LIMIT),
        name="out_proj",
    )(x, yg, ym, w_g, w_m)


def _pad_lanes(v, offset):
    return jnp.zeros((GATE_W,), F32).at[offset:offset + v.shape[0]].set(v.astype(F32))


def kernel(x, ffn1_norm_w, ffn1_w_gate, ffn1_w_up, ffn1_w_down, mix_norm_w, w_in, conv_w, gdn_a_log, gdn_dt_bias, gdn_norm_w, ml_i_bias, ml_f_bias, w_out, ffn2_norm_w, ffn2_w_gate, ffn2_w_up, ffn2_w_down, final_norm_w):
    batch, seq, d = x.shape
    depth = ffn1_norm_w.shape[0]
    h = x.reshape(batch * seq, d)
    fw = final_norm_w.reshape(1, d).astype(F32)
    n_gate = 2 * GDN_HEADS + 2 * ML_HEADS
    g_lo = C_MQ
    m_lo = g_lo + 2 * GDN_HEADS
    m_w = 2 * ML_QK_W + 2 * ML_V_W
    for l in range(depth):
        wi = w_in[l]
        gates = jnp.concatenate([wi[:, g_lo:m_lo], wi[:, m_lo + m_w:],
                                 jnp.zeros((d, GATE_W - n_gate), wi.dtype)], axis=1)
        wi_r = jnp.concatenate([wi[:, :g_lo], wi[:, m_lo:m_lo + m_w], gates], axis=1).astype(BF16)
        wgt = gates.T.astype(BF16)
        bias = _pad_lanes(gdn_dt_bias[l], GA0) + _pad_lanes(ml_i_bias[l], MI0) + _pad_lanes(ml_f_bias[l], MF0)
        alog = _pad_lanes(gdn_a_log[l], GA0)
        prow = jnp.zeros((SUBLANES, GATE_W), F32).at[0].set(bias).at[1].set(alog)
        pcol = jnp.zeros((GATE_W, LANES), F32).at[:, 0].set(bias).at[:, 1].set(alog)
        wo = w_out[l].astype(BF16)

        h = _ffn(h, ffn1_norm_w[l].reshape(1, d), ffn1_w_gate[l].astype(BF16), ffn1_w_up[l].astype(BF16),
                 ffn1_w_down[l].astype(BF16), fw, final_norm=False)
        proj, gt = _in_proj(h, mix_norm_w[l].reshape(1, d), wi_r, wgt)
        yg = _gdn(proj, gt, conv_w[l].astype(F32), prow, pcol, gdn_norm_w[l].reshape(1, GDN_DV).astype(F32),
                  batch=batch, seq=seq)
        ym = _mlstm(proj, gt, prow, pcol, batch=batch, seq=seq)
        h = _out_proj(h, yg, ym, wo[:GDN_V_W], wo[GDN_V_W:])
        h = _ffn(h, ffn2_norm_w[l].reshape(1, d), ffn2_w_gate[l].astype(BF16), ffn2_w_up[l].astype(BF16),
                 ffn2_w_down[l].astype(BF16), fw, final_norm=(l == depth - 1))
    return h.reshape(batch, seq, d)
```

```python
import functools

import jax
import jax.numpy as jnp
from jax import lax
from jax.experimental import pallas as pl
from jax.experimental.pallas import tpu as pltpu

F32 = jnp.float32
BF16 = jnp.bfloat16

D_MODEL = 2048
D_FF = 5632
GDN_HEADS = 8
GDN_DK = 128
GDN_DV = 128
ML_HEADS = 4
ML_DQK = 128
ML_DV = 256
GDN_QK_W = GDN_HEADS * GDN_DK
GDN_V_W = GDN_HEADS * GDN_DV
GDN_QKV_W = 2 * GDN_QK_W + GDN_V_W
ML_QK_W = ML_HEADS * ML_DQK
ML_V_W = ML_HEADS * ML_DV
D_MIX = GDN_V_W + ML_V_W
CONV_WIDTH = 4
CHUNK = 64
NORM_EPS = 1e-6
GATE_SOFTCAP = 15.0

LANES = 128
SUBLANES = 8
GATE_W = LANES
GA0, GB0, MI0, MF0 = 0, GDN_HEADS, 2 * GDN_HEADS, 2 * GDN_HEADS + ML_HEADS
N_GATE = 2 * GDN_HEADS + 2 * ML_HEADS
GDN_W = GDN_QKV_W + GDN_V_W
ML_W = 2 * ML_QK_W + 2 * ML_V_W
C_GZ = GDN_QKV_W
C_MQ = GDN_W
C_MK = C_MQ + ML_QK_W
C_MV = C_MK + ML_QK_W
C_MO = C_MV + ML_V_W
D_PROJ = GDN_W + ML_W

TB = 2 * CHUNK
NCB = TB // CHUNK
NEG = -1e30
VMEM_LIMIT = 56 * 1024 * 1024


def _rms(x, w):
    return x * lax.rsqrt(jnp.mean(x * x, axis=-1, keepdims=True) + NORM_EPS) * w


def _softplus(x):
    return jnp.maximum(x, 0.0) + jnp.log1p(jnp.exp(-jnp.abs(x)))


def _silu(x):
    return x * jax.nn.sigmoid(x)


def _dot(a, b):
    return jnp.dot(a, b, preferred_element_type=F32)


def _dot_nt(a, b, precision=None):
    return lax.dot_general(a, b, (((1,), (1,)), ((), ())), preferred_element_type=F32, precision=precision)


def _dot_hi(a, b):
    return jnp.dot(a, b, preferred_element_type=F32, precision=lax.Precision.HIGHEST)


def _ffn_kernel(x_ref, nw_ref, wg_ref, wu_ref, wd_ref, fw_ref, o_ref, xn_ref, *, nf, final_norm):
    f = pl.program_id(1)

    @pl.when(f == 0)
    def _():
        xn_ref[...] = _rms(x_ref[...], nw_ref[...]).astype(BF16)
        o_ref[...] = jnp.zeros(o_ref.shape, F32)

    tf = wg_ref.shape[1]
    wgu = jnp.concatenate([wg_ref[...].astype(BF16), wu_ref[...].astype(BF16)], axis=1)
    gu = _dot(xn_ref[...], wgu)
    a = (_silu(gu[:, :tf]) * gu[:, tf:]).astype(BF16)
    o_ref[...] += _dot(a, wd_ref[...].astype(BF16))

    @pl.when(f == nf - 1)
    def _():
        y = x_ref[...] + 0.5 * o_ref[...]
        if final_norm:
            y = _rms(y, fw_ref[...])
        o_ref[...] = y


def _ffn(x, nw, wg, wu, wd, fw, *, layer, final_norm, tm=1024, tf=256):
    m, d = x.shape
    dff = wg.shape[2]
    nf = dff // tf
    return pl.pallas_call(
        functools.partial(_ffn_kernel, nf=nf, final_norm=final_norm),
        grid=(m // tm, nf),
        in_specs=[
            pl.BlockSpec((tm, d), lambda i, f: (i, 0), pipeline_mode=pl.Buffered(1)),
            pl.BlockSpec((1, d), lambda i, f: (0, 0)),
            pl.BlockSpec((None, d, tf), lambda i, f: (layer, 0, f)),
            pl.BlockSpec((None, d, tf), lambda i, f: (layer, 0, f)),
            pl.BlockSpec((None, tf, d), lambda i, f: (layer, f, 0)),
            pl.BlockSpec((1, d), lambda i, f: (0, 0)),
        ],
        out_specs=pl.BlockSpec((tm, d), lambda i, f: (i, 0)),
        out_shape=jax.ShapeDtypeStruct((m, d), F32),
        scratch_shapes=[pltpu.VMEM((tm, d), BF16)],
        compiler_params=pltpu.CompilerParams(
            dimension_semantics=("parallel", "arbitrary"), vmem_limit_bytes=VMEM_LIMIT),
        name="ffn",
    )(x, nw, wg, wu, wd, fw)


def _in_proj_kernel(x_ref, nw_ref, wa_ref, wb_ref, wgt_ref, p_ref, gt_ref, xn_ref, *, na):
    j = pl.program_id(1)

    @pl.when(j == 0)
    def _():
        xn = _rms(x_ref[...], nw_ref[...]).astype(BF16)
        xn_ref[...] = xn
        gt_ref[...] = _dot_nt(wgt_ref[...], xn)

    @pl.when(j < na)
    def _():
        p_ref[...] = _dot(xn_ref[...], wa_ref[...].astype(BF16))

    @pl.when(j >= na)
    def _():
        p_ref[...] = _dot(xn_ref[...], wb_ref[...])


def _in_proj(x, nw, w_in, w_ml, wgt, *, layer, tm=1024, tn=1024):
    m, d = x.shape
    na = GDN_W // tn
    nb = ML_W // tn
    return pl.pallas_call(
        functools.partial(_in_proj_kernel, na=na),
        grid=(m // tm, na + nb),
        in_specs=[
            pl.BlockSpec((tm, d), lambda i, j: (i, 0), pipeline_mode=pl.Buffered(1)),
            pl.BlockSpec((1, d), lambda i, j: (0, 0)),
            pl.BlockSpec((None, d, tn), lambda i, j: (layer, 0, jnp.minimum(j, na - 1))),
            pl.BlockSpec((d, tn), lambda i, j: (0, jnp.maximum(j - na, 0))),
            pl.BlockSpec((GATE_W, d), lambda i, j: (0, 0)),
        ],
        out_specs=[
            pl.BlockSpec((tm, tn), lambda i, j: (i, j)),
            pl.BlockSpec((GATE_W, tm), lambda i, j: (0, i)),
        ],
        out_shape=[jax.ShapeDtypeStruct((m, D_PROJ), F32), jax.ShapeDtypeStruct((GATE_W, m), F32)],
        scratch_shapes=[pltpu.VMEM((tm, d), BF16)],
        compiler_params=pltpu.CompilerParams(
            dimension_semantics=("parallel", "arbitrary"), vmem_limit_bytes=VMEM_LIMIT),
        name="in_proj",
    )(x, nw, w_in, w_ml, wgt)


def _chunk_iotas():
    row = lax.broadcasted_iota(jnp.int32, (CHUNK, CHUNK), 0)
    col = lax.broadcasted_iota(jnp.int32, (CHUNK, CHUNK), 1)
    return row, col


def _gate_orientations(rows, row, col):
    tril = (row >= col).astype(F32)
    eye = (row == col).astype(F32)
    triu = (row <= col).astype(F32)
    both = _dot_nt(jnp.concatenate([tril, eye], axis=0), rows, precision=lax.Precision.HIGHEST)
    return _dot_hi(rows, triu), both[:CHUNK], both[CHUNK:]


def _lanes(x, c):
    return x[:, c * CHUNK:(c + 1) * CHUNK]


def _unit_lower_inverse(m, row, col, between):
    strict = row > col
    eye = (row == col).astype(F32)
    first = strict & (jnp.right_shift(row, 1) == jnp.right_shift(col, 1))
    t_inv = [eye - jnp.where(first, mi, 0.0) for mi in m]
    s = 1
    while (2 << s) <= CHUNK:
        join = (strict & (jnp.right_shift(row, s + 1) == jnp.right_shift(col, s + 1))
                & (jnp.right_shift(row, s) != jnp.right_shift(col, s)))
        cb = [jnp.where(join, mi, 0.0).astype(BF16) for mi in m]
        tb = [ti.astype(BF16) for ti in t_inv]
        x = [_dot(ci, ti).astype(BF16) for ci, ti in zip(cb, tb)]
        for fn in between.get(s, ()):
            fn()
        t_inv = [ti - _dot(tbi, xi) for ti, tbi, xi in zip(t_inv, tb, x)]
        for fn in between.get(s + 0.5, ()):
            fn()
        s += 1
    return t_inv


def _mixer_body(src_ref, dst_ref, ext_ref, z_ref, mq_ref, mk_ref, mv_ref, mo_ref, gr_ref, cw_ref,
                pcol_ref, nw_ref, y_ref, s_ref, c_ref, n_ref, m_ref):
    row, col = _chunk_iotas()
    causal = row >= col
    strict = row > col

    def chunk_rows(ref, c, c0, width):
        return ref[c * CHUNK:(c + 1) * CHUNK, c0:c0 + width]

    def conv_tile(lt):
        cols = slice(lt * LANES, (lt + 1) * LANES)
        acc = None
        for tap in range(CONV_WIDTH):
            lo = SUBLANES - (CONV_WIDTH - 1) + tap
            term = ext_ref[lo:lo + TB, cols] * cw_ref[tap:tap + 1, cols]
            acc = term if acc is None else acc + term
        y = _silu(acc)
        if lt < 2 * GDN_HEADS:
            scale = GDN_DK ** -0.5 if lt < GDN_HEADS else 1.0
            y = y * (lax.rsqrt(jnp.sum(y * y, axis=-1, keepdims=True) + NORM_EPS) * scale)
        dst_ref[:, cols] = y

    conv_tiles = [functools.partial(conv_tile, lt) for lt in range(GDN_QKV_W // LANES)]

    def emit_conv(count):
        for _ in range(min(count, len(conv_tiles))):
            conv_tiles.pop(0)()

    g_row = (-jnp.exp(pcol_ref[GA0:GA0 + GDN_HEADS, 1:2])
             * _softplus(gr_ref[GA0:GA0 + GDN_HEADS, :] + pcol_ref[GA0:GA0 + GDN_HEADS, 0:1]))
    beta_row = jax.nn.sigmoid(gr_ref[GB0:GB0 + GDN_HEADS, :])
    g_gates = jnp.concatenate([g_row, beta_row], axis=0)
    pre = gr_ref[MI0:MI0 + 2 * ML_HEADS, :] + pcol_ref[MI0:MI0 + 2 * ML_HEADS, 0:1]
    capped = GATE_SOFTCAP * jnp.tanh(pre / GATE_SOFTCAP)
    is_i = lax.broadcasted_iota(jnp.int32, capped.shape, 0) < ML_HEADS
    m_gates = jnp.where(is_i, capped, -_softplus(-capped))
    g_go = [_gate_orientations(_lanes(g_gates, c), row, col) for c in range(NCB)]
    m_go = [_gate_orientations(_lanes(m_gates, c), row, col) for c in range(NCB)]


    heads = range(GDN_HEADS)
    probs = [(c, h) for c in range(NCB) for h in heads]
    q = [chunk_rows(src_ref, c, h * GDN_DK, GDN_DK) for c, h in probs]
    k = [chunk_rows(src_ref, c, GDN_QK_W + h * GDN_DK, GDN_DK) for c, h in probs]
    vb = [chunk_rows(src_ref, c, 2 * GDN_QK_W + h * GDN_DV, GDN_DV).astype(BF16) for c, h in probs]
    kb = [ki.astype(BF16) for ki in k]
    emit_conv(3)
    kt = [ki.T for ki in k]
    emit_conv(3)
    gc_r = [g_go[c][0][h:h + 1, :] for c, h in probs]
    gc_c = [g_go[c][1][:, h:h + 1] for c, h in probs]
    beta_c = [g_go[c][2][:, GB0 + h:GB0 + h + 1] for c, h in probs]
    beta_r = [_lanes(beta_row, c)[h:h + 1, :] for c, h in probs]
    decay = [jnp.exp(jnp.where(causal, a - b, NEG)) for a, b in zip(gc_c, gc_r)]
    kq = [_dot(jnp.concatenate([kbi, qi.astype(BF16)], axis=0), kti.astype(BF16))
          for kbi, qi, kti in zip(kb, q, kt)]
    emit_conv(3)
    m = [jnp.where(strict, x[:CHUNK] * d * b, 0.0) for x, d, b in zip(kq, decay, beta_c)]
    attn = [(x[CHUNK:] * d).astype(BF16) for x, d in zip(kq, decay)]

    mheads = range(ML_HEADS)
    mprobs = [(c, h) for c in range(NCB) for h in mheads]
    mq = [chunk_rows(mq_ref, c, h * ML_DQK, ML_DQK) * (ML_DQK ** -0.5) for c, h in mprobs]
    mk = [chunk_rows(mk_ref, c, h * ML_DQK, ML_DQK) for c, h in mprobs]
    mkt = [x.T for x in mk]
    emit_conv(3)
    mqb = [x.astype(BF16) for x in mq]
    mvb = [chunk_rows(mv_ref, c, h * ML_DV, ML_DV).astype(BF16) for c, h in mprobs]
    b_r = [m_go[c][0][ML_HEADS + h:ML_HEADS + h + 1, :] for c, h in mprobs]
    b_c = [m_go[c][1][:, ML_HEADS + h:ML_HEADS + h + 1] for c, h in mprobs]
    i_c = [m_go[c][2][:, h:h + 1] for c, h in mprobs]
    i_r = [_lanes(m_gates, c)[h:h + 1, :] for c, h in mprobs]
    b_end = [x[:, CHUNK - 1:CHUNK] for x in b_r]
    dmat = [jnp.where(causal, bc - br + ir, NEG) for bc, br, ir in zip(b_c, b_r, i_r)]
    m_intra = [jnp.max(d, axis=-1, keepdims=True) for d in dmat]
    mqk = [_dot(a, b.astype(BF16)) for a, b in zip(mqb, mkt)]

    ml_state = {
        "c": [c_ref[h] for h in mheads],
        "n": [n_ref[h:h + 1, :] for h in mheads],
        "m": [m_ref[h:h + 1, 0:1] for h in mheads],
    }

    def mlstm_chunk(c):
        c_st, n_st, m_st = ml_state["c"], ml_state["n"], ml_state["m"]
        ids = [c * ML_HEADS + h for h in mheads]
        inter_log = [b_c[i] + m_st[h] for h, i in zip(mheads, ids)]
        m_t = [jnp.maximum(a, m_intra[i]) for a, i in zip(inter_log, ids)]
        inter = [jnp.exp(a - b) for a, b in zip(inter_log, m_t)]
        p = [jnp.exp(dmat[i] - mt) * mqk[i] for mt, i in zip(m_t, ids)]
        num = [it * _dot(mqb[i], c_st[h].astype(BF16)) + _dot(pi.astype(BF16), mvb[i])
               for h, i, it, pi in zip(mheads, ids, inter, p)]
        den = [it * jnp.sum(mq[i] * n_st[h], axis=-1, keepdims=True) + jnp.sum(pi, axis=-1, keepdims=True)
               for h, i, it, pi in zip(mheads, ids, inter, p)]
        hout = [nu / jnp.maximum(jnp.abs(de), jnp.exp(-mt)) for nu, de, mt in zip(num, den, m_t)]
        m_new = [mt[CHUNK - 1:CHUNK, :] for mt in m_t]
        carry = [jnp.exp(b_end[i] + m_st[h] - mn) for h, i, mn in zip(mheads, ids, m_new)]
        ktw = [(mkt[i] * jnp.exp(b_end[i] - b_r[i] + i_r[i] - mn)).astype(BF16) for i, mn in zip(ids, m_new)]
        kw = [mk[i] * jnp.exp(b_end[i] - b_c[i] + i_c[i] - mn) for i, mn in zip(ids, m_new)]
        ml_state["c"] = [ca * c_st[h] + _dot(x, mvb[i]) for h, i, ca, x in zip(mheads, ids, carry, ktw)]
        ml_state["n"] = [ca * n_st[h] + jnp.sum(x, axis=0, keepdims=True) for h, ca, x in zip(mheads, carry, kw)]
        ml_state["m"] = m_new
        for h, ho in zip(mheads, hout):
            og = chunk_rows(mo_ref, c, h * ML_DV, ML_DV)
            y_ref[c * CHUNK:(c + 1) * CHUNK, GDN_V_W + h * ML_DV:GDN_V_W + (h + 1) * ML_DV] = (
                jax.nn.sigmoid(og) * ho).astype(y_ref.dtype)

    between = {}
    slots = [1, 1.5, 2, 2.5, 3, 3.5, 4, 4.5, 5, 5.5]
    for n in range(len(conv_tiles)):
        between.setdefault(slots[n % len(slots)], []).append(functools.partial(emit_conv, 1))
    for c in range(NCB):
        between.setdefault(2 + c, []).append(functools.partial(mlstm_chunk, c))
    t_inv = _unit_lower_inverse(m, row, col, between)
    for h in mheads:
        c_ref[h] = ml_state["c"][h]
        n_ref[h:h + 1, :] = ml_state["n"][h]
        m_ref[h:h + 1, :] = jnp.broadcast_to(ml_state["m"][h], (1, LANES))

    tbeta = [ti * b for ti, b in zip(t_inv, beta_r)]
    u = [_dot(x.astype(BF16), vi) for x, vi in zip(tbeta, vb)]
    w = [_dot((x * jnp.exp(g)).astype(BF16), kbi) for x, g, kbi in zip(tbeta, gc_r, kb)]
    gc_last = [g[:, CHUNK - 1:CHUNK] for g in gc_r]
    kt_dec = [(kti * jnp.exp(gl - g)).astype(BF16) for kti, gl, g in zip(kt, gc_last, gc_r)]
    g_end = [jnp.exp(gl) for gl in gc_last]
    wq_lhs = [jnp.concatenate([wi, qi * jnp.exp(g)], axis=0).astype(BF16) for wi, qi, g in zip(w, q, gc_c)]

    st = [s_ref[h] for h in heads]
    o = []
    for c in range(NCB):
        i0 = c * GDN_HEADS
        wq = [_dot(wq_lhs[i0 + h], st[h].astype(BF16)) for h in heads]
        vn = [(u[i0 + h] - wq[h][:CHUNK]).astype(BF16) for h in heads]
        o += [wq[h][CHUNK:] + _dot(attn[i0 + h], vn[h]) for h in heads]
        st = [g_end[i0 + h] * st[h] + _dot(kt_dec[i0 + h], vn[h]) for h in heads]
    for h in heads:
        s_ref[h] = st[h]

    nw = nw_ref[...]
    for (c, h), oi in zip(probs, o):
        oi = oi * lax.rsqrt(jnp.mean(oi * oi, axis=-1, keepdims=True) + NORM_EPS) * nw
        oi = oi * _silu(chunk_rows(z_ref, c, h * GDN_DV, GDN_DV))
        y_ref[c * CHUNK:(c + 1) * CHUNK, h * GDN_DV:(h + 1) * GDN_DV] = oi.astype(y_ref.dtype)


def _mixer_kernel(raw_ref, z_ref, mq_ref, mk_ref, mv_ref, mo_ref, gr_ref, cw_ref, pcol_ref, nw_ref,
                  y_ref, ext_ref, buf0_ref, buf1_ref, s_ref, c_ref, n_ref, m_ref):
    t = pl.program_id(1)

    @pl.when(t == 0)
    def _():
        ext_ref[0:SUBLANES, :] = jnp.zeros((SUBLANES, ext_ref.shape[1]), F32)
        buf1_ref[...] = jnp.zeros(buf1_ref.shape, F32)

    @pl.when(t <= 1)
    def _():
        s_ref[...] = jnp.zeros(s_ref.shape, F32)
        c_ref[...] = jnp.zeros(c_ref.shape, F32)
        n_ref[...] = jnp.zeros(n_ref.shape, F32)
        m_ref[...] = jnp.zeros(m_ref.shape, F32)

    ext_ref[SUBLANES:SUBLANES + TB, :] = raw_ref[...]
    rest = (ext_ref, z_ref, mq_ref, mk_ref, mv_ref, mo_ref, gr_ref, cw_ref, pcol_ref, nw_ref,
            y_ref, s_ref, c_ref, n_ref, m_ref)

    @pl.when(t % 2 == 0)
    def _():
        _mixer_body(buf1_ref, buf0_ref, *rest)

    @pl.when(t % 2 == 1)
    def _():
        _mixer_body(buf0_ref, buf1_ref, *rest)

    ext_ref[0:SUBLANES, :] = ext_ref[TB:TB + SUBLANES, :]


def _mixer(proj, gt, conv_w, pcol, norm_w, *, batch, seq):
    nt = seq // TB

    def nxt(b, t):
        return b * nt + jnp.minimum(t, nt - 1)

    def cur(b, t):
        return b * nt + jnp.maximum(t - 1, 0)

    return pl.pallas_call(
        _mixer_kernel,
        grid=(batch, nt + 1),
        in_specs=[
            pl.BlockSpec((TB, GDN_QKV_W), lambda b, t: (nxt(b, t), 0)),
            pl.BlockSpec((TB, GDN_V_W), lambda b, t: (cur(b, t), C_GZ // GDN_V_W)),
            pl.BlockSpec((TB, ML_QK_W), lambda b, t: (cur(b, t), C_MQ // ML_QK_W)),
            pl.BlockSpec((TB, ML_QK_W), lambda b, t: (cur(b, t), C_MK // ML_QK_W)),
            pl.BlockSpec((TB, ML_V_W), lambda b, t: (cur(b, t), C_MV // ML_V_W)),
            pl.BlockSpec((TB, ML_V_W), lambda b, t: (cur(b, t), C_MO // ML_V_W)),
            pl.BlockSpec((GATE_W, TB), lambda b, t: (0, cur(b, t))),
            pl.BlockSpec((CONV_WIDTH, GDN_QKV_W), lambda b, t: (0, 0)),
            pl.BlockSpec((GATE_W, LANES), lambda b, t: (0, 0)),
            pl.BlockSpec((1, GDN_DV), lambda b, t: (0, 0)),
        ],
        out_specs=pl.BlockSpec((TB, D_MIX), lambda b, t: (cur(b, t), 0)),
        out_shape=jax.ShapeDtypeStruct((batch * seq, D_MIX), BF16),
        scratch_shapes=[
            pltpu.VMEM((TB + SUBLANES, GDN_QKV_W), F32),
            pltpu.VMEM((TB, GDN_QKV_W), F32),
            pltpu.VMEM((TB, GDN_QKV_W), F32),
            pltpu.VMEM((GDN_HEADS, GDN_DK, GDN_DV), F32),
            pltpu.VMEM((ML_HEADS, ML_DQK, ML_DV), F32),
            pltpu.VMEM((SUBLANES, ML_DQK), F32),
            pltpu.VMEM((SUBLANES, LANES), F32),
        ],
        compiler_params=pltpu.CompilerParams(
            dimension_semantics=("parallel", "arbitrary"), vmem_limit_bytes=VMEM_LIMIT),
        name="mixer",
    )(proj, proj, proj, proj, proj, proj, gt, conv_w, pcol, norm_w)


def _out_proj_kernel(x_ref, y_ref, w_ref, o_ref):
    o_ref[...] = x_ref[...] + _dot(y_ref[...], w_ref[...])


def _out_proj(x, y, w, *, tm=512):
    m, d = x.shape
    return pl.pallas_call(
        _out_proj_kernel,
        grid=(m // tm,),
        in_specs=[
            pl.BlockSpec((tm, d), lambda i: (i, 0)),
            pl.BlockSpec((tm, D_MIX), lambda i: (i, 0)),
            pl.BlockSpec((D_MIX, d), lambda i: (0, 0)),
        ],
        out_specs=pl.BlockSpec((tm, d), lambda i: (i, 0)),
        out_shape=jax.ShapeDtypeStruct((m, d), F32),
        compiler_params=pltpu.CompilerParams(
            dimension_semantics=("parallel",), vmem_limit_bytes=VMEM_LIMIT),
        name="out_proj",
    )(x, y, w)


def _pad_rows(v, offset):
    return jnp.zeros((GATE_W,), F32).at[offset:offset + v.shape[0]].set(v.astype(F32))


def kernel(x, ffn1_norm_w, ffn1_w_gate, ffn1_w_up, ffn1_w_down, mix_norm_w, w_in, conv_w, gdn_a_log, gdn_dt_bias, gdn_norm_w, ml_i_bias, ml_f_bias, w_out, ffn2_norm_w, ffn2_w_gate, ffn2_w_up, ffn2_w_down, final_norm_w):
    batch, seq, d = x.shape
    depth = ffn1_norm_w.shape[0]
    h = x.reshape(batch * seq, d)
    fw = final_norm_w.reshape(1, d).astype(F32)
    m_lo = GDN_W + 2 * GDN_HEADS
    for l in range(depth):
        wi = w_in[l]
        w_ml = wi[:, m_lo:m_lo + ML_W].astype(BF16)
        gates = jnp.concatenate([wi[:, GDN_W:m_lo], wi[:, m_lo + ML_W:]], axis=1)
        wgt = jnp.zeros((GATE_W, d), BF16).at[:N_GATE].set(gates.T.astype(BF16))
        bias = _pad_rows(gdn_dt_bias[l], GA0) + _pad_rows(ml_i_bias[l], MI0) + _pad_rows(ml_f_bias[l], MF0)
        alog = _pad_rows(gdn_a_log[l], GA0)
        pcol = jnp.zeros((GATE_W, LANES), F32).at[:, 0].set(bias).at[:, 1].set(alog)

        h = _ffn(h, ffn1_norm_w[l].reshape(1, d), ffn1_w_gate, ffn1_w_up, ffn1_w_down, fw,
                 layer=l, final_norm=False)
        proj, gt = _in_proj(h, mix_norm_w[l].reshape(1, d), w_in, w_ml, wgt, layer=l)
        y = _mixer(proj, gt, conv_w[l].astype(F32), pcol, gdn_norm_w[l].reshape(1, GDN_DV).astype(F32),
                   batch=batch, seq=seq)
        h = _out_proj(h, y, w_out[l].astype(BF16))
        h = _ffn(h, ffn2_norm_w[l].reshape(1, d), ffn2_w_gate, ffn2_w_up, ffn2_w_down, fw,
                 layer=l, final_norm=(l == depth - 1))
    return h.reshape(batch, seq, d)
```

```python
import functools

import jax
import jax.numpy as jnp
from jax import lax
from jax.experimental import pallas as pl
from jax.experimental.pallas import tpu as pltpu

F32 = jnp.float32
BF16 = jnp.bfloat16

D_MODEL = 2048
D_FF = 5632
GDN_HEADS = 8
GDN_DK = 128
GDN_DV = 128
ML_HEADS = 4
ML_DQK = 128
ML_DV = 256
GDN_QK_W = GDN_HEADS * GDN_DK
GDN_V_W = GDN_HEADS * GDN_DV
GDN_QKV_W = 2 * GDN_QK_W + GDN_V_W
ML_QK_W = ML_HEADS * ML_DQK
ML_V_W = ML_HEADS * ML_DV
D_MIX = GDN_V_W + ML_V_W
CONV_WIDTH = 4
CHUNK = 64
NORM_EPS = 1e-6
GATE_SOFTCAP = 15.0

LANES = 128
SUBLANES = 8
GATE_W = LANES
GA0, GB0, MI0, MF0 = 0, GDN_HEADS, 2 * GDN_HEADS, 2 * GDN_HEADS + ML_HEADS
N_GATE = 2 * GDN_HEADS + 2 * ML_HEADS
GDN_W = GDN_QKV_W + GDN_V_W
ML_W = 2 * ML_QK_W + 2 * ML_V_W
C_GZ = GDN_QKV_W
C_MQ = GDN_W
C_MK = C_MQ + ML_QK_W
C_MV = C_MK + ML_QK_W
C_MO = C_MV + ML_V_W
D_PROJ = GDN_W + ML_W

TB = 2 * CHUNK
NCB = TB // CHUNK
NEG = -1e30
VMEM_LIMIT = 60 * 1024 * 1024


def _rms(x, w):
    return x * lax.rsqrt(jnp.mean(x * x, axis=-1, keepdims=True) + NORM_EPS) * w


def _softplus(x):
    return jnp.maximum(x, 0.0) + jnp.log1p(jnp.exp(-jnp.abs(x)))


def _silu(x):
    return x * jax.nn.sigmoid(x)


def _dot(a, b):
    return jnp.dot(a, b, preferred_element_type=F32)


def _dot_nt(a, b, precision=None):
    return lax.dot_general(a, b, (((1,), (1,)), ((), ())), preferred_element_type=F32, precision=precision)


def _dot_hi(a, b):
    return jnp.dot(a, b, preferred_element_type=F32, precision=lax.Precision.HIGHEST)


def _ffn_kernel(x_ref, nw_ref, wg_ref, wu_ref, wd_ref, fw_ref, o_ref, xn_ref, *, nf, final_norm):
    f = pl.program_id(1)

    @pl.when(f == 0)
    def _():
        xn_ref[...] = _rms(x_ref[...], nw_ref[...]).astype(BF16)
        o_ref[...] = jnp.zeros(o_ref.shape, F32)

    tf = wg_ref.shape[1]
    wgu = jnp.concatenate([wg_ref[...].astype(BF16), wu_ref[...].astype(BF16)], axis=1)
    gu = _dot(xn_ref[...], wgu)
    a = (_silu(gu[:, :tf]) * gu[:, tf:]).astype(BF16)
    o_ref[...] += _dot(a, wd_ref[...].astype(BF16))

    @pl.when(f == nf - 1)
    def _():
        y = x_ref[...] + 0.5 * o_ref[...]
        if final_norm:
            y = _rms(y, fw_ref[...])
        o_ref[...] = y


def _ffn(x, nw, wg, wu, wd, fw, *, layer, final_norm, tm=1024, tf=256):
    m, d = x.shape
    dff = wg.shape[2]
    nf = dff // tf
    return pl.pallas_call(
        functools.partial(_ffn_kernel, nf=nf, final_norm=final_norm),
        grid=(m // tm, nf),
        in_specs=[
            pl.BlockSpec((tm, d), lambda i, f: (i, 0)),
            pl.BlockSpec((1, d), lambda i, f: (0, 0)),
            pl.BlockSpec((None, d, tf), lambda i, f: (layer, 0, f)),
            pl.BlockSpec((None, d, tf), lambda i, f: (layer, 0, f)),
            pl.BlockSpec((None, tf, d), lambda i, f: (layer, f, 0)),
            pl.BlockSpec((1, d), lambda i, f: (0, 0)),
        ],
        out_specs=pl.BlockSpec((tm, d), lambda i, f: (i, 0)),
        out_shape=jax.ShapeDtypeStruct((m, d), F32),
        scratch_shapes=[pltpu.VMEM((tm, d), BF16)],
        compiler_params=pltpu.CompilerParams(
            dimension_semantics=("parallel", "arbitrary"), vmem_limit_bytes=VMEM_LIMIT),
        name="ffn",
    )(x, nw, wg, wu, wd, fw)


def _in_proj_kernel(x_ref, nw_ref, w_ref, wgt_ref, p_ref, gt_ref, xn_ref):
    j = pl.program_id(1)

    @pl.when(j == 0)
    def _():
        xn = _rms(x_ref[...], nw_ref[...]).astype(BF16)
        xn_ref[...] = xn
        gt_ref[...] = _dot_nt(wgt_ref[...], xn)

    p_ref[...] = _dot(xn_ref[...], w_ref[...].astype(BF16))


def _in_proj(x, nw, w, wgt, *, tm=1024, tn=1024):
    m, d = x.shape
    n = w.shape[1]
    return pl.pallas_call(
        _in_proj_kernel,
        grid=(m // tm, n // tn),
        in_specs=[
            pl.BlockSpec((tm, d), lambda i, j: (i, 0)),
            pl.BlockSpec((1, d), lambda i, j: (0, 0)),
            pl.BlockSpec((d, tn), lambda i, j: (0, j)),
            pl.BlockSpec((GATE_W, d), lambda i, j: (0, 0)),
        ],
        out_specs=[
            pl.BlockSpec((tm, tn), lambda i, j: (i, j)),
            pl.BlockSpec((GATE_W, tm), lambda i, j: (0, i)),
        ],
        out_shape=[jax.ShapeDtypeStruct((m, n), F32), jax.ShapeDtypeStruct((GATE_W, m), F32)],
        scratch_shapes=[pltpu.VMEM((tm, d), BF16)],
        compiler_params=pltpu.CompilerParams(
            dimension_semantics=("parallel", "arbitrary"), vmem_limit_bytes=VMEM_LIMIT),
        name="in_proj",
    )(x, nw, w, wgt)


def _chunk_iotas():
    row = lax.broadcasted_iota(jnp.int32, (CHUNK, CHUNK), 0)
    col = lax.broadcasted_iota(jnp.int32, (CHUNK, CHUNK), 1)
    return row, col


def _gate_orientations(rows, row, col):
    tril = (row >= col).astype(F32)
    eye = (row == col).astype(F32)
    triu = (row <= col).astype(F32)
    both = _dot_nt(jnp.concatenate([tril, eye], axis=0), rows, precision=lax.Precision.HIGHEST)
    return _dot_hi(rows, triu), both[:CHUNK], both[CHUNK:]


def _lanes(x, c):
    return x[:, c * CHUNK:(c + 1) * CHUNK]


def _unit_lower_inverse(m, row, col, between):
    strict = row > col
    eye = (row == col).astype(F32)
    first = strict & (jnp.right_shift(row, 1) == jnp.right_shift(col, 1))
    t_inv = [eye - jnp.where(first, mi, 0.0) for mi in m]
    s = 1
    while (2 << s) <= CHUNK:
        join = (strict & (jnp.right_shift(row, s + 1) == jnp.right_shift(col, s + 1))
                & (jnp.right_shift(row, s) != jnp.right_shift(col, s)))
        cb = [jnp.where(join, mi, 0.0).astype(BF16) for mi in m]
        tb = [ti.astype(BF16) for ti in t_inv]
        x = [_dot(ci, ti).astype(BF16) for ci, ti in zip(cb, tb)]
        for fn in between.get(s, ()):
            fn()
        t_inv = [ti - _dot(tbi, xi) for ti, tbi, xi in zip(t_inv, tb, x)]
        for fn in between.get(s + 0.5, ()):
            fn()
        s += 1
    return t_inv


def _mixer_body(src_ref, dst_ref, ext_ref, z_ref, mq_ref, mk_ref, mv_ref, mo_ref, gr_ref, cw_ref,
                pcol_ref, nw_ref, y_ref, s_ref, c_ref, n_ref, m_ref):
    row, col = _chunk_iotas()
    causal = row >= col
    strict = row > col

    def chunk_rows(ref, c, c0, width):
        return ref[c * CHUNK:(c + 1) * CHUNK, c0:c0 + width]

    def conv_tile(lt):
        cols = slice(lt * LANES, (lt + 1) * LANES)
        acc = None
        for tap in range(CONV_WIDTH):
            lo = SUBLANES - (CONV_WIDTH - 1) + tap
            term = ext_ref[lo:lo + TB, cols] * cw_ref[tap:tap + 1, cols]
            acc = term if acc is None else acc + term
        y = _silu(acc)
        if lt < 2 * GDN_HEADS:
            scale = GDN_DK ** -0.5 if lt < GDN_HEADS else 1.0
            y = y * (lax.rsqrt(jnp.sum(y * y, axis=-1, keepdims=True) + NORM_EPS) * scale)
        dst_ref[:, cols] = y

    conv_tiles = [functools.partial(conv_tile, lt) for lt in range(GDN_QKV_W // LANES)]

    def emit_conv(count):
        for _ in range(min(count, len(conv_tiles))):
            conv_tiles.pop(0)()

    g_row = (-jnp.exp(pcol_ref[GA0:GA0 + GDN_HEADS, 1:2])
             * _softplus(gr_ref[GA0:GA0 + GDN_HEADS, :] + pcol_ref[GA0:GA0 + GDN_HEADS, 0:1]))
    beta_row = jax.nn.sigmoid(gr_ref[GB0:GB0 + GDN_HEADS, :])
    g_gates = jnp.concatenate([g_row, beta_row], axis=0)
    pre = gr_ref[MI0:MI0 + 2 * ML_HEADS, :] + pcol_ref[MI0:MI0 + 2 * ML_HEADS, 0:1]
    capped = GATE_SOFTCAP * jnp.tanh(pre / GATE_SOFTCAP)
    is_i = lax.broadcasted_iota(jnp.int32, capped.shape, 0) < ML_HEADS
    m_gates = jnp.where(is_i, capped, -_softplus(-capped))
    g_go = [_gate_orientations(_lanes(g_gates, c), row, col) for c in range(NCB)]
    m_go = [_gate_orientations(_lanes(m_gates, c), row, col) for c in range(NCB)]


    heads = range(GDN_HEADS)
    probs = [(c, h) for c in range(NCB) for h in heads]
    q = [chunk_rows(src_ref, c, h * GDN_DK, GDN_DK) for c, h in probs]
    k = [chunk_rows(src_ref, c, GDN_QK_W + h * GDN_DK, GDN_DK) for c, h in probs]
    vb = [chunk_rows(src_ref, c, 2 * GDN_QK_W + h * GDN_DV, GDN_DV).astype(BF16) for c, h in probs]
    kb = [ki.astype(BF16) for ki in k]
    emit_conv(3)
    kt = [ki.T for ki in k]
    emit_conv(3)
    gc_r = [g_go[c][0][h:h + 1, :] for c, h in probs]
    gc_c = [g_go[c][1][:, h:h + 1] for c, h in probs]
    beta_c = [g_go[c][2][:, GB0 + h:GB0 + h + 1] for c, h in probs]
    beta_r = [_lanes(beta_row, c)[h:h + 1, :] for c, h in probs]
    decay = [jnp.exp(jnp.where(causal, a - b, NEG)) for a, b in zip(gc_c, gc_r)]
    kq = [_dot(jnp.concatenate([kbi, qi.astype(BF16)], axis=0), kti.astype(BF16))
          for kbi, qi, kti in zip(kb, q, kt)]
    emit_conv(3)
    m = [jnp.where(strict, x[:CHUNK] * d * b, 0.0) for x, d, b in zip(kq, decay, beta_c)]
    attn = [(x[CHUNK:] * d).astype(BF16) for x, d in zip(kq, decay)]

    mheads = range(ML_HEADS)
    mprobs = [(c, h) for c in range(NCB) for h in mheads]
    mq = [chunk_rows(mq_ref, c, h * ML_DQK, ML_DQK) * (ML_DQK ** -0.5) for c, h in mprobs]
    mk = [chunk_rows(mk_ref, c, h * ML_DQK, ML_DQK) for c, h in mprobs]
    mkt = [x.T for x in mk]
    emit_conv(3)
    mqb = [x.astype(BF16) for x in mq]
    mvb = [chunk_rows(mv_ref, c, h * ML_DV, ML_DV).astype(BF16) for c, h in mprobs]
    b_r = [m_go[c][0][ML_HEADS + h:ML_HEADS + h + 1, :] for c, h in mprobs]
    b_c = [m_go[c][1][:, ML_HEADS + h:ML_HEADS + h + 1] for c, h in mprobs]
    i_c = [m_go[c][2][:, h:h + 1] for c, h in mprobs]
    i_r = [_lanes(m_gates, c)[h:h + 1, :] for c, h in mprobs]
    b_end = [x[:, CHUNK - 1:CHUNK] for x in b_r]
    dmat = [jnp.where(causal, bc - br + ir, NEG) for bc, br, ir in zip(b_c, b_r, i_r)]
    m_intra = [jnp.max(d, axis=-1, keepdims=True) for d in dmat]
    mqk = [_dot(a, b.astype(BF16)) for a, b in zip(mqb, mkt)]

    ml_state = {
        "c": [c_ref[h] for h in mheads],
        "n": [n_ref[h:h + 1, :] for h in mheads],
        "m": [m_ref[h:h + 1, 0:1] for h in mheads],
    }

    def mlstm_chunk(c):
        c_st, n_st, m_st = ml_state["c"], ml_state["n"], ml_state["m"]
        ids = [c * ML_HEADS + h for h in mheads]
        inter_log = [b_c[i] + m_st[h] for h, i in zip(mheads, ids)]
        m_t = [jnp.maximum(a, m_intra[i]) for a, i in zip(inter_log, ids)]
        inter = [jnp.exp(a - b) for a, b in zip(inter_log, m_t)]
        p = [jnp.exp(dmat[i] - mt) * mqk[i] for mt, i in zip(m_t, ids)]
        num = [it * _dot(mqb[i], c_st[h].astype(BF16)) + _dot(pi.astype(BF16), mvb[i])
               for h, i, it, pi in zip(mheads, ids, inter, p)]
        den = [it * jnp.sum(mq[i] * n_st[h], axis=-1, keepdims=True) + jnp.sum(pi, axis=-1, keepdims=True)
               for h, i, it, pi in zip(mheads, ids, inter, p)]
        hout = [nu / jnp.maximum(jnp.abs(de), jnp.exp(-mt)) for nu, de, mt in zip(num, den, m_t)]
        m_new = [mt[CHUNK - 1:CHUNK, :] for mt in m_t]
        carry = [jnp.exp(b_end[i] + m_st[h] - mn) for h, i, mn in zip(mheads, ids, m_new)]
        ktw = [(mkt[i] * jnp.exp(b_end[i] - b_r[i] + i_r[i] - mn)).astype(BF16) for i, mn in zip(ids, m_new)]
        kw = [mk[i] * jnp.exp(b_end[i] - b_c[i] + i_c[i] - mn) for i, mn in zip(ids, m_new)]
        ml_state["c"] = [ca * c_st[h] + _dot(x, mvb[i]) for h, i, ca, x in zip(mheads, ids, carry, ktw)]
        ml_state["n"] = [ca * n_st[h] + jnp.sum(x, axis=0, keepdims=True) for h, ca, x in zip(mheads, carry, kw)]
        ml_state["m"] = m_new
        for h, ho in zip(mheads, hout):
            og = chunk_rows(mo_ref, c, h * ML_DV, ML_DV)
            y_ref[c * CHUNK:(c + 1) * CHUNK, GDN_V_W + h * ML_DV:GDN_V_W + (h + 1) * ML_DV] = (
                jax.nn.sigmoid(og) * ho).astype(y_ref.dtype)

    between = {}
    slots = [1, 1.5, 2, 2.5, 3, 3.5, 4, 4.5, 5, 5.5]
    for n in range(len(conv_tiles)):
        between.setdefault(slots[n % len(slots)], []).append(functools.partial(emit_conv, 1))
    for c in range(NCB):
        between.setdefault(2 + c, []).append(functools.partial(mlstm_chunk, c))
    t_inv = _unit_lower_inverse(m, row, col, between)
    for h in mheads:
        c_ref[h] = ml_state["c"][h]
        n_ref[h:h + 1, :] = ml_state["n"][h]
        m_ref[h:h + 1, :] = jnp.broadcast_to(ml_state["m"][h], (1, LANES))

    tbeta = [ti * b for ti, b in zip(t_inv, beta_r)]
    u = [_dot(x.astype(BF16), vi) for x, vi in zip(tbeta, vb)]
    w = [_dot((x * jnp.exp(g)).astype(BF16), kbi) for x, g, kbi in zip(tbeta, gc_r, kb)]
    gc_last = [g[:, CHUNK - 1:CHUNK] for g in gc_r]
    kt_dec = [(kti * jnp.exp(gl - g)).astype(BF16) for kti, gl, g in zip(kt, gc_last, gc_r)]
    g_end = [jnp.exp(gl) for gl in gc_last]
    wq_lhs = [jnp.concatenate([wi, qi * jnp.exp(g)], axis=0).astype(BF16) for wi, qi, g in zip(w, q, gc_c)]

    st = [s_ref[h] for h in heads]
    o = []
    for c in range(NCB):
        i0 = c * GDN_HEADS
        wq = [_dot(wq_lhs[i0 + h], st[h].astype(BF16)) for h in heads]
        vn = [(u[i0 + h] - wq[h][:CHUNK]).astype(BF16) for h in heads]
        o += [wq[h][CHUNK:] + _dot(attn[i0 + h], vn[h]) for h in heads]
        st = [g_end[i0 + h] * st[h] + _dot(kt_dec[i0 + h], vn[h]) for h in heads]
    for h in heads:
        s_ref[h] = st[h]

    nw = nw_ref[...]
    for (c, h), oi in zip(probs, o):
        oi = oi * lax.rsqrt(jnp.mean(oi * oi, axis=-1, keepdims=True) + NORM_EPS) * nw
        oi = oi * _silu(chunk_rows(z_ref, c, h * GDN_DV, GDN_DV))
        y_ref[c * CHUNK:(c + 1) * CHUNK, h * GDN_DV:(h + 1) * GDN_DV] = oi.astype(y_ref.dtype)


def _mixer_kernel(raw_ref, z_ref, mq_ref, mk_ref, mv_ref, mo_ref, gr_ref, cw_ref, pcol_ref, nw_ref,
                  y_ref, ext_ref, buf0_ref, buf1_ref, s_ref, c_ref, n_ref, m_ref):
    t = pl.program_id(1)

    @pl.when(t == 0)
    def _():
        ext_ref[0:SUBLANES, :] = jnp.zeros((SUBLANES, ext_ref.shape[1]), F32)
        buf1_ref[...] = jnp.zeros(buf1_ref.shape, F32)

    @pl.when(t <= 1)
    def _():
        s_ref[...] = jnp.zeros(s_ref.shape, F32)
        c_ref[...] = jnp.zeros(c_ref.shape, F32)
        n_ref[...] = jnp.zeros(n_ref.shape, F32)
        m_ref[...] = jnp.zeros(m_ref.shape, F32)

    ext_ref[SUBLANES:SUBLANES + TB, :] = raw_ref[...]
    rest = (ext_ref, z_ref, mq_ref, mk_ref, mv_ref, mo_ref, gr_ref, cw_ref, pcol_ref, nw_ref,
            y_ref, s_ref, c_ref, n_ref, m_ref)

    @pl.when(t % 2 == 0)
    def _():
        _mixer_body(buf1_ref, buf0_ref, *rest)

    @pl.when(t % 2 == 1)
    def _():
        _mixer_body(buf0_ref, buf1_ref, *rest)

    ext_ref[0:SUBLANES, :] = ext_ref[TB:TB + SUBLANES, :]


def _mixer(proj, gt, conv_w, pcol, norm_w, *, batch, seq):
    nt = seq // TB

    def nxt(b, t):
        return b * nt + jnp.minimum(t, nt - 1)

    def cur(b, t):
        return b * nt + jnp.maximum(t - 1, 0)

    return pl.pallas_call(
        _mixer_kernel,
        grid=(batch, nt + 1),
        in_specs=[
            pl.BlockSpec((TB, GDN_QKV_W), lambda b, t: (nxt(b, t), 0)),
            pl.BlockSpec((TB, GDN_V_W), lambda b, t: (cur(b, t), C_GZ // GDN_V_W)),
            pl.BlockSpec((TB, ML_QK_W), lambda b, t: (cur(b, t), C_MQ // ML_QK_W)),
            pl.BlockSpec((TB, ML_QK_W), lambda b, t: (cur(b, t), C_MK // ML_QK_W)),
            pl.BlockSpec((TB, ML_V_W), lambda b, t: (cur(b, t), C_MV // ML_V_W)),
            pl.BlockSpec((TB, ML_V_W), lambda b, t: (cur(b, t), C_MO // ML_V_W)),
            pl.BlockSpec((GATE_W, TB), lambda b, t: (0, cur(b, t))),
            pl.BlockSpec((CONV_WIDTH, GDN_QKV_W), lambda b, t: (0, 0)),
            pl.BlockSpec((GATE_W, LANES), lambda b, t: (0, 0)),
            pl.BlockSpec((1, GDN_DV), lambda b, t: (0, 0)),
        ],
        out_specs=pl.BlockSpec((TB, D_MIX), lambda b, t: (cur(b, t), 0)),
        out_shape=jax.ShapeDtypeStruct((batch * seq, D_MIX), BF16),
        scratch_shapes=[
            pltpu.VMEM((TB + SUBLANES, GDN_QKV_W), F32),
            pltpu.VMEM((TB, GDN_QKV_W), F32),
            pltpu.VMEM((TB, GDN_QKV_W), F32),
            pltpu.VMEM((GDN_HEADS, GDN_DK, GDN_DV), F32),
            pltpu.VMEM((ML_HEADS, ML_DQK, ML_DV), F32),
            pltpu.VMEM((SUBLANES, ML_DQK), F32),
            pltpu.VMEM((SUBLANES, LANES), F32),
        ],
        compiler_params=pltpu.CompilerParams(
            dimension_semantics=("parallel", "arbitrary"), vmem_limit_bytes=VMEM_LIMIT),
        name="mixer",
    )(proj, proj, proj, proj, proj, proj, gt, conv_w, pcol, norm_w)


def _out_proj_kernel(x_ref, y_ref, w_ref, o_ref):
    o_ref[...] = x_ref[...] + _dot(y_ref[...], w_ref[...])


def _out_proj(x, y, w, *, tm=512):
    m, d = x.shape
    return pl.pallas_call(
        _out_proj_kernel,
        grid=(m // tm,),
        in_specs=[
            pl.BlockSpec((tm, d), lambda i: (i, 0)),
            pl.BlockSpec((tm, D_MIX), lambda i: (i, 0)),
            pl.BlockSpec((D_MIX, d), lambda i: (0, 0)),
        ],
        out_specs=pl.BlockSpec((tm, d), lambda i: (i, 0)),
        out_shape=jax.ShapeDtypeStruct((m, d), F32),
        compiler_params=pltpu.CompilerParams(
            dimension_semantics=("parallel",), vmem_limit_bytes=VMEM_LIMIT),
        name="out_proj",
    )(x, y, w)


def _pad_rows(v, offset):
    return jnp.zeros((GATE_W,), F32).at[offset:offset + v.shape[0]].set(v.astype(F32))


def kernel(x, ffn1_norm_w, ffn1_w_gate, ffn1_w_up, ffn1_w_down, mix_norm_w, w_in, conv_w, gdn_a_log, gdn_dt_bias, gdn_norm_w, ml_i_bias, ml_f_bias, w_out, ffn2_norm_w, ffn2_w_gate, ffn2_w_up, ffn2_w_down, final_norm_w):
    batch, seq, d = x.shape
    depth = ffn1_norm_w.shape[0]
    h = x.reshape(batch * seq, d)
    fw = final_norm_w.reshape(1, d).astype(F32)
    m_lo = GDN_W + 2 * GDN_HEADS
    for l in range(depth):
        wi = w_in[l]
        w_proj = jnp.concatenate([wi[:, :GDN_W], wi[:, m_lo:m_lo + ML_W]], axis=1)
        gates = jnp.concatenate([wi[:, GDN_W:m_lo], wi[:, m_lo + ML_W:]], axis=1)
        wgt = jnp.zeros((GATE_W, d), BF16).at[:N_GATE].set(gates.T.astype(BF16))
        bias = _pad_rows(gdn_dt_bias[l], GA0) + _pad_rows(ml_i_bias[l], MI0) + _pad_rows(ml_f_bias[l], MF0)
        alog = _pad_rows(gdn_a_log[l], GA0)
        pcol = jnp.zeros((GATE_W, LANES), F32).at[:, 0].set(bias).at[:, 1].set(alog)

        h = _ffn(h, ffn1_norm_w[l].reshape(1, d), ffn1_w_gate, ffn1_w_up, ffn1_w_down, fw,
                 layer=l, final_norm=False)
        proj, gt = _in_proj(h, mix_norm_w[l].reshape(1, d), w_proj, wgt)
        y = _mixer(proj, gt, conv_w[l].astype(F32), pcol, gdn_norm_w[l].reshape(1, GDN_DV).astype(F32),
                   batch=batch, seq=seq)
        h = _out_proj(h, y, w_out[l].astype(BF16))
        h = _ffn(h, ffn2_norm_w[l].reshape(1, d), ffn2_w_gate, ffn2_w_up, ffn2_w_down, fw,
                 layer=l, final_norm=(l == depth - 1))
    return h.reshape(batch, seq, d)
```

```python
import functools

import jax
import jax.numpy as jnp
from jax import lax
from jax.experimental import pallas as pl
from jax.experimental.pallas import tpu as pltpu

F32 = jnp.float32
BF16 = jnp.bfloat16

D_MODEL = 2048
D_FF = 5632
GDN_HEADS = 8
GDN_DK = 128
GDN_DV = 128
ML_HEADS = 4
ML_DQK = 128
ML_DV = 256
GDN_QK_W = GDN_HEADS * GDN_DK
GDN_V_W = GDN_HEADS * GDN_DV
GDN_QKV_W = 2 * GDN_QK_W + GDN_V_W
ML_QK_W = ML_HEADS * ML_DQK
ML_V_W = ML_HEADS * ML_DV
D_MIX = GDN_V_W + ML_V_W
CONV_WIDTH = 4
CHUNK = 64
NORM_EPS = 1e-6
GATE_SOFTCAP = 15.0

LANES = 128
SUBLANES = 8
GATE_W = LANES
GA0, GB0, MI0, MF0 = 0, GDN_HEADS, 2 * GDN_HEADS, 2 * GDN_HEADS + ML_HEADS
N_GATE = 2 * GDN_HEADS + 2 * ML_HEADS
GDN_W = GDN_QKV_W + GDN_V_W
ML_W = 2 * ML_QK_W + 2 * ML_V_W
C_GZ = GDN_QKV_W
C_MQ = GDN_W
C_MK = C_MQ + ML_QK_W
C_MV = C_MK + ML_QK_W
C_MO = C_MV + ML_V_W
D_PROJ = GDN_W + ML_W

TB = 2 * CHUNK
NCB = TB // CHUNK
NEG = -1e30
VMEM_LIMIT = 60 * 1024 * 1024


def _rms(x, w):
    return x * lax.rsqrt(jnp.mean(x * x, axis=-1, keepdims=True) + NORM_EPS) * w


def _softplus(x):
    return jnp.maximum(x, 0.0) + jnp.log1p(jnp.exp(-jnp.abs(x)))


def _silu(x):
    return x * jax.nn.sigmoid(x)


def _dot(a, b):
    return jnp.dot(a, b, preferred_element_type=F32)


def _dot_nt(a, b, precision=None):
    return lax.dot_general(a, b, (((1,), (1,)), ((), ())), preferred_element_type=F32, precision=precision)


def _dot_hi(a, b):
    return jnp.dot(a, b, preferred_element_type=F32, precision=lax.Precision.HIGHEST)


def _ffn_kernel(x_ref, nw_ref, wg_ref, wu_ref, wd_ref, fw_ref, o_ref, xn_ref, *, nf, final_norm):
    f = pl.program_id(1)

    @pl.when(f == 0)
    def _():
        xn_ref[...] = _rms(x_ref[...], nw_ref[...]).astype(BF16)
        o_ref[...] = jnp.zeros(o_ref.shape, F32)

    tf = wg_ref.shape[1]
    wgu = jnp.concatenate([wg_ref[...].astype(BF16), wu_ref[...].astype(BF16)], axis=1)
    gu = _dot(xn_ref[...], wgu)
    a = (_silu(gu[:, :tf]) * gu[:, tf:]).astype(BF16)
    o_ref[...] += _dot(a, wd_ref[...].astype(BF16))

    @pl.when(f == nf - 1)
    def _():
        y = x_ref[...] + 0.5 * o_ref[...]
        if final_norm:
            y = _rms(y, fw_ref[...])
        o_ref[...] = y


def _ffn(x, nw, wg, wu, wd, fw, *, layer, final_norm, tm=1024, tf=256):
    m, d = x.shape
    dff = wg.shape[2]
    nf = dff // tf
    return pl.pallas_call(
        functools.partial(_ffn_kernel, nf=nf, final_norm=final_norm),
        grid=(m // tm, nf),
        in_specs=[
            pl.BlockSpec((tm, d), lambda i, f: (i, 0)),
            pl.BlockSpec((1, d), lambda i, f: (0, 0)),
            pl.BlockSpec((None, d, tf), lambda i, f: (layer, 0, f)),
            pl.BlockSpec((None, d, tf), lambda i, f: (layer, 0, f)),
            pl.BlockSpec((None, tf, d), lambda i, f: (layer, f, 0)),
            pl.BlockSpec((1, d), lambda i, f: (0, 0)),
        ],
        out_specs=pl.BlockSpec((tm, d), lambda i, f: (i, 0)),
        out_shape=jax.ShapeDtypeStruct((m, d), F32),
        scratch_shapes=[pltpu.VMEM((tm, d), BF16)],
        compiler_params=pltpu.CompilerParams(
            dimension_semantics=("parallel", "arbitrary"), vmem_limit_bytes=VMEM_LIMIT),
        name="ffn",
    )(x, nw, wg, wu, wd, fw)


def _in_proj_kernel(x_ref, nw_ref, w_ref, wx_ref, wgt_ref, p_ref, gt_ref, xn_ref, *, na, shift):
    j = pl.program_id(1)

    @pl.when(j == 0)
    def _():
        xn = _rms(x_ref[...], nw_ref[...]).astype(BF16)
        xn_ref[...] = xn
        gt_ref[...] = _dot_nt(wgt_ref[...], xn)

    @pl.when(j < na)
    def _():
        p_ref[...] = _dot(xn_ref[...], w_ref[...])

    @pl.when(j >= na)
    def _():
        tn = w_ref.shape[1]
        w = jnp.concatenate([w_ref[...], wx_ref[...]], axis=1)
        w = pltpu.bitcast(pltpu.bitcast(w, jnp.uint32)[:, shift:shift + tn], BF16)
        p_ref[...] = _dot(xn_ref[...], w)


def _in_proj(x, nw, w, wgt, *, tm=1024, tn=1024):
    m, d = x.shape
    na = GDN_W // tn
    nb = ML_W // tn
    return pl.pallas_call(
        functools.partial(_in_proj_kernel, na=na, shift=2 * GDN_HEADS),
        grid=(m // tm, na + nb),
        in_specs=[
            pl.BlockSpec((tm, d), lambda i, j: (i, 0)),
            pl.BlockSpec((1, d), lambda i, j: (0, 0)),
            pl.BlockSpec((d, tn), lambda i, j: (0, j)),
            pl.BlockSpec((d, LANES), lambda i, j: (0, (j + 1) * (tn // LANES))),
            pl.BlockSpec((GATE_W, d), lambda i, j: (0, 0)),
        ],
        out_specs=[
            pl.BlockSpec((tm, tn), lambda i, j: (i, j)),
            pl.BlockSpec((GATE_W, tm), lambda i, j: (0, i)),
        ],
        out_shape=[jax.ShapeDtypeStruct((m, D_PROJ), F32), jax.ShapeDtypeStruct((GATE_W, m), F32)],
        scratch_shapes=[pltpu.VMEM((tm, d), BF16)],
        compiler_params=pltpu.CompilerParams(
            dimension_semantics=("parallel", "arbitrary"), vmem_limit_bytes=VMEM_LIMIT),
        name="in_proj",
    )(x, nw, w, w, wgt)


def _chunk_iotas():
    row = lax.broadcasted_iota(jnp.int32, (CHUNK, CHUNK), 0)
    col = lax.broadcasted_iota(jnp.int32, (CHUNK, CHUNK), 1)
    return row, col


def _gate_orientations(rows, row, col):
    tril = (row >= col).astype(F32)
    eye = (row == col).astype(F32)
    triu = (row <= col).astype(F32)
    both = _dot_nt(jnp.concatenate([tril, eye], axis=0), rows, precision=lax.Precision.HIGHEST)
    return _dot_hi(rows, triu), both[:CHUNK], both[CHUNK:]


def _lanes(x, c):
    return x[:, c * CHUNK:(c + 1) * CHUNK]


def _unit_lower_inverse(m, row, col, between):
    strict = row > col
    eye = (row == col).astype(F32)
    first = strict & (jnp.right_shift(row, 1) == jnp.right_shift(col, 1))
    t_inv = [eye - jnp.where(first, mi, 0.0) for mi in m]
    s = 1
    while (2 << s) <= CHUNK:
        join = (strict & (jnp.right_shift(row, s + 1) == jnp.right_shift(col, s + 1))
                & (jnp.right_shift(row, s) != jnp.right_shift(col, s)))
        cb = [jnp.where(join, mi, 0.0).astype(BF16) for mi in m]
        tb = [ti.astype(BF16) for ti in t_inv]
        x = [_dot(ci, ti).astype(BF16) for ci, ti in zip(cb, tb)]
        for fn in between.get(s, ()):
            fn()
        t_inv = [ti - _dot(tbi, xi) for ti, tbi, xi in zip(t_inv, tb, x)]
        for fn in between.get(s + 0.5, ()):
            fn()
        s += 1
    return t_inv


def _mixer_body(src_ref, dst_ref, ext_ref, z_ref, mq_ref, mk_ref, mv_ref, mo_ref, gr_ref, cw_ref,
                pcol_ref, nw_ref, y_ref, s_ref, c_ref, n_ref, m_ref):
    row, col = _chunk_iotas()
    causal = row >= col
    strict = row > col

    def chunk_rows(ref, c, c0, width):
        return ref[c * CHUNK:(c + 1) * CHUNK, c0:c0 + width]

    def conv_tile(lt):
        cols = slice(lt * LANES, (lt + 1) * LANES)
        acc = None
        for tap in range(CONV_WIDTH):
            lo = SUBLANES - (CONV_WIDTH - 1) + tap
            term = ext_ref[lo:lo + TB, cols] * cw_ref[tap:tap + 1, cols]
            acc = term if acc is None else acc + term
        y = _silu(acc)
        if lt < 2 * GDN_HEADS:
            scale = GDN_DK ** -0.5 if lt < GDN_HEADS else 1.0
            y = y * (lax.rsqrt(jnp.sum(y * y, axis=-1, keepdims=True) + NORM_EPS) * scale)
        dst_ref[:, cols] = y

    conv_tiles = [functools.partial(conv_tile, lt) for lt in range(GDN_QKV_W // LANES)]

    def emit_conv(count):
        for _ in range(min(count, len(conv_tiles))):
            conv_tiles.pop(0)()

    g_row = (-jnp.exp(pcol_ref[GA0:GA0 + GDN_HEADS, 1:2])
             * _softplus(gr_ref[GA0:GA0 + GDN_HEADS, :] + pcol_ref[GA0:GA0 + GDN_HEADS, 0:1]))
    beta_row = jax.nn.sigmoid(gr_ref[GB0:GB0 + GDN_HEADS, :])
    g_gates = jnp.concatenate([g_row, beta_row], axis=0)
    pre = gr_ref[MI0:MI0 + 2 * ML_HEADS, :] + pcol_ref[MI0:MI0 + 2 * ML_HEADS, 0:1]
    capped = GATE_SOFTCAP * jnp.tanh(pre / GATE_SOFTCAP)
    is_i = lax.broadcasted_iota(jnp.int32, capped.shape, 0) < ML_HEADS
    m_gates = jnp.where(is_i, capped, -_softplus(-capped))
    g_go = [_gate_orientations(_lanes(g_gates, c), row, col) for c in range(NCB)]
    m_go = [_gate_orientations(_lanes(m_gates, c), row, col) for c in range(NCB)]


    heads = range(GDN_HEADS)
    probs = [(c, h) for c in range(NCB) for h in heads]
    q = [chunk_rows(src_ref, c, h * GDN_DK, GDN_DK) for c, h in probs]
    k = [chunk_rows(src_ref, c, GDN_QK_W + h * GDN_DK, GDN_DK) for c, h in probs]
    vb = [chunk_rows(src_ref, c, 2 * GDN_QK_W + h * GDN_DV, GDN_DV).astype(BF16) for c, h in probs]
    kb = [ki.astype(BF16) for ki in k]
    emit_conv(3)
    kt = [ki.T for ki in k]
    emit_conv(3)
    gc_r = [g_go[c][0][h:h + 1, :] for c, h in probs]
    gc_c = [g_go[c][1][:, h:h + 1] for c, h in probs]
    beta_c = [g_go[c][2][:, GB0 + h:GB0 + h + 1] for c, h in probs]
    beta_r = [_lanes(beta_row, c)[h:h + 1, :] for c, h in probs]
    decay = [jnp.exp(jnp.where(causal, a - b, NEG)) for a, b in zip(gc_c, gc_r)]
    kq = [_dot(jnp.concatenate([kbi, qi.astype(BF16)], axis=0), kti.astype(BF16))
          for kbi, qi, kti in zip(kb, q, kt)]
    emit_conv(3)
    m = [jnp.where(strict, x[:CHUNK] * d * b, 0.0) for x, d, b in zip(kq, decay, beta_c)]
    attn = [(x[CHUNK:] * d).astype(BF16) for x, d in zip(kq, decay)]

    mheads = range(ML_HEADS)
    mprobs = [(c, h) for c in range(NCB) for h in mheads]
    mq = [chunk_rows(mq_ref, c, h * ML_DQK, ML_DQK) * (ML_DQK ** -0.5) for c, h in mprobs]
    mk = [chunk_rows(mk_ref, c, h * ML_DQK, ML_DQK) for c, h in mprobs]
    mkt = [x.T for x in mk]
    emit_conv(3)
    mqb = [x.astype(BF16) for x in mq]
    mvb = [chunk_rows(mv_ref, c, h * ML_DV, ML_DV).astype(BF16) for c, h in mprobs]
    b_r = [m_go[c][0][ML_HEADS + h:ML_HEADS + h + 1, :] for c, h in mprobs]
    b_c = [m_go[c][1][:, ML_HEADS + h:ML_HEADS + h + 1] for c, h in mprobs]
    i_c = [m_go[c][2][:, h:h + 1] for c, h in mprobs]
    i_r = [_lanes(m_gates, c)[h:h + 1, :] for c, h in mprobs]
    b_end = [x[:, CHUNK - 1:CHUNK] for x in b_r]
    dmat = [jnp.where(causal, bc - br + ir, NEG) for bc, br, ir in zip(b_c, b_r, i_r)]
    m_intra = [jnp.max(d, axis=-1, keepdims=True) for d in dmat]
    mqk = [_dot(a, b.astype(BF16)) for a, b in zip(mqb, mkt)]

    ml_state = {
        "c": [c_ref[h] for h in mheads],
        "n": [n_ref[h:h + 1, :] for h in mheads],
        "m": [m_ref[h:h + 1, 0:1] for h in mheads],
    }

    def mlstm_chunk(c):
        c_st, n_st, m_st = ml_state["c"], ml_state["n"], ml_state["m"]
        ids = [c * ML_HEADS + h for h in mheads]
        inter_log = [b_c[i] + m_st[h] for h, i in zip(mheads, ids)]
        m_t = [jnp.maximum(a, m_intra[i]) for a, i in zip(inter_log, ids)]
        inter = [jnp.exp(a - b) for a, b in zip(inter_log, m_t)]
        p = [jnp.exp(dmat[i] - mt) * mqk[i] for mt, i in zip(m_t, ids)]
        num = [it * _dot(mqb[i], c_st[h].astype(BF16)) + _dot(pi.astype(BF16), mvb[i])
               for h, i, it, pi in zip(mheads, ids, inter, p)]
        den = [it * jnp.sum(mq[i] * n_st[h], axis=-1, keepdims=True) + jnp.sum(pi, axis=-1, keepdims=True)
               for h, i, it, pi in zip(mheads, ids, inter, p)]
        hout = [nu / jnp.maximum(jnp.abs(de), jnp.exp(-mt)) for nu, de, mt in zip(num, den, m_t)]
        m_new = [mt[CHUNK - 1:CHUNK, :] for mt in m_t]
        carry = [jnp.exp(b_end[i] + m_st[h] - mn) for h, i, mn in zip(mheads, ids, m_new)]
        ktw = [(mkt[i] * jnp.exp(b_end[i] - b_r[i] + i_r[i] - mn)).astype(BF16) for i, mn in zip(ids, m_new)]
        kw = [mk[i] * jnp.exp(b_end[i] - b_c[i] + i_c[i] - mn) for i, mn in zip(ids, m_new)]
        ml_state["c"] = [ca * c_st[h] + _dot(x, mvb[i]) for h, i, ca, x in zip(mheads, ids, carry, ktw)]
        ml_state["n"] = [ca * n_st[h] + jnp.sum(x, axis=0, keepdims=True) for h, ca, x in zip(mheads, carry, kw)]
        ml_state["m"] = m_new
        for h, ho in zip(mheads, hout):
            og = chunk_rows(mo_ref, c, h * ML_DV, ML_DV)
            y_ref[c * CHUNK:(c + 1) * CHUNK, GDN_V_W + h * ML_DV:GDN_V_W + (h + 1) * ML_DV] = (
                jax.nn.sigmoid(og) * ho).astype(y_ref.dtype)

    between = {}
    slots = [1, 1.5, 2, 2.5, 3, 3.5, 4, 4.5, 5, 5.5]
    for n in range(len(conv_tiles)):
        between.setdefault(slots[n % len(slots)], []).append(functools.partial(emit_conv, 1))
    for c in range(NCB):
        between.setdefault(2 + c, []).append(functools.partial(mlstm_chunk, c))
    t_inv = _unit_lower_inverse(m, row, col, between)
    for h in mheads:
        c_ref[h] = ml_state["c"][h]
        n_ref[h:h + 1, :] = ml_state["n"][h]
        m_ref[h:h + 1, :] = jnp.broadcast_to(ml_state["m"][h], (1, LANES))

    tbeta = [ti * b for ti, b in zip(t_inv, beta_r)]
    u = [_dot(x.astype(BF16), vi) for x, vi in zip(tbeta, vb)]
    w = [_dot((x * jnp.exp(g)).astype(BF16), kbi) for x, g, kbi in zip(tbeta, gc_r, kb)]
    gc_last = [g[:, CHUNK - 1:CHUNK] for g in gc_r]
    kt_dec = [(kti * jnp.exp(gl - g)).astype(BF16) for kti, gl, g in zip(kt, gc_last, gc_r)]
    g_end = [jnp.exp(gl) for gl in gc_last]
    wq_lhs = [jnp.concatenate([wi, qi * jnp.exp(g)], axis=0).astype(BF16) for wi, qi, g in zip(w, q, gc_c)]

    st = [s_ref[h] for h in heads]
    o = []
    for c in range(NCB):
        i0 = c * GDN_HEADS
        wq = [_dot(wq_lhs[i0 + h], st[h].astype(BF16)) for h in heads]
        vn = [(u[i0 + h] - wq[h][:CHUNK]).astype(BF16) for h in heads]
        o += [wq[h][CHUNK:] + _dot(attn[i0 + h], vn[h]) for h in heads]
        st = [g_end[i0 + h] * st[h] + _dot(kt_dec[i0 + h], vn[h]) for h in heads]
    for h in heads:
        s_ref[h] = st[h]

    nw = nw_ref[...]
    for (c, h), oi in zip(probs, o):
        oi = oi * lax.rsqrt(jnp.mean(oi * oi, axis=-1, keepdims=True) + NORM_EPS) * nw
        oi = oi * _silu(chunk_rows(z_ref, c, h * GDN_DV, GDN_DV))
        y_ref[c * CHUNK:(c + 1) * CHUNK, h * GDN_DV:(h + 1) * GDN_DV] = oi.astype(y_ref.dtype)


def _mixer_kernel(raw_ref, z_ref, mq_ref, mk_ref, mv_ref, mo_ref, gr_ref, cw_ref, pcol_ref, nw_ref,
                  y_ref, ext_ref, buf0_ref, buf1_ref, s_ref, c_ref, n_ref, m_ref):
    t = pl.program_id(1)

    @pl.when(t == 0)
    def _():
        ext_ref[0:SUBLANES, :] = jnp.zeros((SUBLANES, ext_ref.shape[1]), F32)
        buf1_ref[...] = jnp.zeros(buf1_ref.shape, F32)

    @pl.when(t <= 1)
    def _():
        s_ref[...] = jnp.zeros(s_ref.shape, F32)
        c_ref[...] = jnp.zeros(c_ref.shape, F32)
        n_ref[...] = jnp.zeros(n_ref.shape, F32)
        m_ref[...] = jnp.zeros(m_ref.shape, F32)

    ext_ref[SUBLANES:SUBLANES + TB, :] = raw_ref[...]
    rest = (ext_ref, z_ref, mq_ref, mk_ref, mv_ref, mo_ref, gr_ref, cw_ref, pcol_ref, nw_ref,
            y_ref, s_ref, c_ref, n_ref, m_ref)

    @pl.when(t % 2 == 0)
    def _():
        _mixer_body(buf1_ref, buf0_ref, *rest)

    @pl.when(t % 2 == 1)
    def _():
        _mixer_body(buf0_ref, buf1_ref, *rest)

    ext_ref[0:SUBLANES, :] = ext_ref[TB:TB + SUBLANES, :]


def _mixer(proj, gt, conv_w, pcol, norm_w, *, batch, seq):
    nt = seq // TB

    def nxt(b, t):
        return b * nt + jnp.minimum(t, nt - 1)

    def cur(b, t):
        return b * nt + jnp.maximum(t - 1, 0)

    return pl.pallas_call(
        _mixer_kernel,
        grid=(batch, nt + 1),
        in_specs=[
            pl.BlockSpec((TB, GDN_QKV_W), lambda b, t: (nxt(b, t), 0)),
            pl.BlockSpec((TB, GDN_V_W), lambda b, t: (cur(b, t), C_GZ // GDN_V_W)),
            pl.BlockSpec((TB, ML_QK_W), lambda b, t: (cur(b, t), C_MQ // ML_QK_W)),
            pl.BlockSpec((TB, ML_QK_W), lambda b, t: (cur(b, t), C_MK // ML_QK_W)),
            pl.BlockSpec((TB, ML_V_W), lambda b, t: (cur(b, t), C_MV // ML_V_W)),
            pl.BlockSpec((TB, ML_V_W), lambda b, t: (cur(b, t), C_MO // ML_V_W)),
            pl.BlockSpec((GATE_W, TB), lambda b, t: (0, cur(b, t))),
            pl.BlockSpec((CONV_WIDTH, GDN_QKV_W), lambda b, t: (0, 0)),
            pl.BlockSpec((GATE_W, LANES), lambda b, t: (0, 0)),
            pl.BlockSpec((1, GDN_DV), lambda b, t: (0, 0)),
        ],
        out_specs=pl.BlockSpec((TB, D_MIX), lambda b, t: (cur(b, t), 0)),
        out_shape=jax.ShapeDtypeStruct((batch * seq, D_MIX), BF16),
        scratch_shapes=[
            pltpu.VMEM((TB + SUBLANES, GDN_QKV_W), F32),
            pltpu.VMEM((TB, GDN_QKV_W), F32),
            pltpu.VMEM((TB, GDN_QKV_W), F32),
            pltpu.VMEM((GDN_HEADS, GDN_DK, GDN_DV), F32),
            pltpu.VMEM((ML_HEADS, ML_DQK, ML_DV), F32),
            pltpu.VMEM((SUBLANES, ML_DQK), F32),
            pltpu.VMEM((SUBLANES, LANES), F32),
        ],
        compiler_params=pltpu.CompilerParams(
            dimension_semantics=("parallel", "arbitrary"), vmem_limit_bytes=VMEM_LIMIT),
        name="mixer",
    )(proj, proj, proj, proj, proj, proj, gt, conv_w, pcol, norm_w)


def _out_proj_kernel(x_ref, y_ref, w_ref, o_ref):
    o_ref[...] = x_ref[...] + _dot(y_ref[...], w_ref[...])


def _out_proj(x, y, w, *, tm=512):
    m, d = x.shape
    return pl.pallas_call(
        _out_proj_kernel,
        grid=(m // tm,),
        in_specs=[
            pl.BlockSpec((tm, d), lambda i: (i, 0)),
            pl.BlockSpec((tm, D_MIX), lambda i: (i, 0)),
            pl.BlockSpec((D_MIX, d), lambda i: (0, 0)),
        ],
        out_specs=pl.BlockSpec((tm, d), lambda i: (i, 0)),
        out_shape=jax.ShapeDtypeStruct((m, d), F32),
        compiler_params=pltpu.CompilerParams(
            dimension_semantics=("parallel",), vmem_limit_bytes=VMEM_LIMIT),
        name="out_proj",
    )(x, y, w)


def _pad_rows(v, offset):
    return jnp.zeros((GATE_W,), F32).at[offset:offset + v.shape[0]].set(v.astype(F32))


def kernel(x, ffn1_norm_w, ffn1_w_gate, ffn1_w_up, ffn1_w_down, mix_norm_w, w_in, conv_w, gdn_a_log, gdn_dt_bias, gdn_norm_w, ml_i_bias, ml_f_bias, w_out, ffn2_norm_w, ffn2_w_gate, ffn2_w_up, ffn2_w_down, final_norm_w):
    batch, seq, d = x.shape
    depth = ffn1_norm_w.shape[0]
    h = x.reshape(batch * seq, d)
    fw = final_norm_w.reshape(1, d).astype(F32)
    m_lo = GDN_W + 2 * GDN_HEADS
    for l in range(depth):
        d_in = w_in.shape[2]
        w_proj = jnp.pad(w_in[l], ((0, 0), (0, -d_in % LANES))).astype(BF16)
        gates = jnp.concatenate([w_proj[:, GDN_W:m_lo], w_proj[:, m_lo + ML_W:d_in]], axis=1)
        wgt = jnp.zeros((GATE_W, d), BF16).at[:N_GATE].set(gates.T)
        bias = _pad_rows(gdn_dt_bias[l], GA0) + _pad_rows(ml_i_bias[l], MI0) + _pad_rows(ml_f_bias[l], MF0)
        alog = _pad_rows(gdn_a_log[l], GA0)
        pcol = jnp.zeros((GATE_W, LANES), F32).at[:, 0].set(bias).at[:, 1].set(alog)

        h = _ffn(h, ffn1_norm_w[l].reshape(1, d), ffn1_w_gate, ffn1_w_up, ffn1_w_down, fw,
                 layer=l, final_norm=False)
        proj, gt = _in_proj(h, mix_norm_w[l].reshape(1, d), w_proj, wgt)
        y = _mixer(proj, gt, conv_w[l].astype(F32), pcol, gdn_norm_w[l].reshape(1, GDN_DV).astype(F32),
                   batch=batch, seq=seq)
        h = _out_proj(h, y, w_out[l].astype(BF16))
        h = _ffn(h, ffn2_norm_w[l].reshape(1, d), ffn2_w_gate, ffn2_w_up, ffn2_w_down, fw,
                 layer=l, final_norm=(l == depth - 1))
    return h.reshape(batch, seq, d)
```

```python
import functools

import jax
import jax.numpy as jnp
from jax import lax
from jax.experimental import pallas as pl
from jax.experimental.pallas import tpu as pltpu

F32 = jnp.float32
BF16 = jnp.bfloat16

D_MODEL = 2048
D_FF = 5632
GDN_HEADS = 8
GDN_DK = 128
GDN_DV = 128
ML_HEADS = 4
ML_DQK = 128
ML_DV = 256
GDN_QK_W = GDN_HEADS * GDN_DK
GDN_V_W = GDN_HEADS * GDN_DV
GDN_QKV_W = 2 * GDN_QK_W + GDN_V_W
ML_QK_W = ML_HEADS * ML_DQK
ML_V_W = ML_HEADS * ML_DV
D_MIX = GDN_V_W + ML_V_W
CONV_WIDTH = 4
CHUNK = 64
NORM_EPS = 1e-6
GATE_SOFTCAP = 15.0

LANES = 128
SUBLANES = 8
GATE_W = LANES
GA0, GB0, MI0, MF0 = 0, GDN_HEADS, 2 * GDN_HEADS, 2 * GDN_HEADS + ML_HEADS
N_GATE = 2 * GDN_HEADS + 2 * ML_HEADS
GDN_W = GDN_QKV_W + GDN_V_W
ML_W = 2 * ML_QK_W + 2 * ML_V_W
C_GZ = GDN_QKV_W
C_MQ = GDN_W
C_MK = C_MQ + ML_QK_W
C_MV = C_MK + ML_QK_W
C_MO = C_MV + ML_V_W
D_PROJ = GDN_W + ML_W

TB = 2 * CHUNK
NCB = TB // CHUNK
CONV_STRIDE = 4
NEG = -1e30
VMEM_LIMIT = 60 * 1024 * 1024


def _rms(x, w):
    return x * lax.rsqrt(jnp.mean(x * x, axis=-1, keepdims=True) + NORM_EPS) * w


def _softplus(x):
    return jnp.maximum(x, 0.0) + jnp.log1p(jnp.exp(-jnp.abs(x)))


def _silu(x):
    return x * jax.nn.sigmoid(x)


def _dot(a, b):
    return jnp.dot(a, b, preferred_element_type=F32)


def _dot_nt(a, b, precision=None):
    return lax.dot_general(a, b, (((1,), (1,)), ((), ())), preferred_element_type=F32, precision=precision)


def _dot_hi(a, b):
    return jnp.dot(a, b, preferred_element_type=F32, precision=lax.Precision.HIGHEST)


def _ffn_kernel(x_ref, nw_ref, wg_ref, wu_ref, wd_ref, fw_ref, o_ref, xn_ref, *, nf, final_norm):
    f = pl.program_id(1)

    @pl.when(f == 0)
    def _():
        xn_ref[...] = _rms(x_ref[...], nw_ref[...]).astype(BF16)
        o_ref[...] = jnp.zeros(o_ref.shape, F32)

    tf = wg_ref.shape[1]
    wgu = jnp.concatenate([wg_ref[...].astype(BF16), wu_ref[...].astype(BF16)], axis=1)
    gu = _dot(xn_ref[...], wgu)
    a = (_silu(gu[:, :tf]) * gu[:, tf:]).astype(BF16)
    o_ref[...] += _dot(a, wd_ref[...].astype(BF16))

    @pl.when(f == nf - 1)
    def _():
        y = x_ref[...] + 0.5 * o_ref[...]
        if final_norm:
            y = _rms(y, fw_ref[...])
        o_ref[...] = y


def _ffn(x, nw, wg, wu, wd, fw, *, layer, final_norm, tm=1024, tf=256):
    m, d = x.shape
    dff = wg.shape[2]
    nf = dff // tf
    return pl.pallas_call(
        functools.partial(_ffn_kernel, nf=nf, final_norm=final_norm),
        grid=(m // tm, nf),
        in_specs=[
            pl.BlockSpec((tm, d), lambda i, f: (i, 0)),
            pl.BlockSpec((1, d), lambda i, f: (0, 0)),
            pl.BlockSpec((None, d, tf), lambda i, f: (layer, 0, f)),
            pl.BlockSpec((None, d, tf), lambda i, f: (layer, 0, f)),
            pl.BlockSpec((None, tf, d), lambda i, f: (layer, f, 0)),
            pl.BlockSpec((1, d), lambda i, f: (0, 0)),
        ],
        out_specs=pl.BlockSpec((tm, d), lambda i, f: (i, 0)),
        out_shape=jax.ShapeDtypeStruct((m, d), F32),
        scratch_shapes=[pltpu.VMEM((tm, d), BF16)],
        compiler_params=pltpu.CompilerParams(
            dimension_semantics=("parallel", "arbitrary"), vmem_limit_bytes=VMEM_LIMIT),
        name="ffn",
    )(x, nw, wg, wu, wd, fw)


def _in_proj_kernel(x_ref, nw_ref, w_ref, wx_ref, wgt_ref, p_ref, gt_ref, xn_ref, *, na, shift):
    j = pl.program_id(1)

    @pl.when(j == 0)
    def _():
        xn = _rms(x_ref[...], nw_ref[...]).astype(BF16)
        xn_ref[...] = xn
        gt_ref[...] = _dot_nt(wgt_ref[...], xn)

    @pl.when(j < na)
    def _():
        p_ref[...] = _dot(xn_ref[...], w_ref[...])

    @pl.when(j >= na)
    def _():
        tn = w_ref.shape[1]
        w = jnp.concatenate([w_ref[...], wx_ref[...]], axis=1)
        w = pltpu.bitcast(pltpu.bitcast(w, jnp.uint32)[:, shift:shift + tn], BF16)
        p_ref[...] = _dot(xn_ref[...], w)


def _in_proj(x, nw, w, wgt, *, tm=1024, tn=1024):
    m, d = x.shape
    na = GDN_W // tn
    nb = ML_W // tn
    return pl.pallas_call(
        functools.partial(_in_proj_kernel, na=na, shift=2 * GDN_HEADS),
        grid=(m // tm, na + nb),
        in_specs=[
            pl.BlockSpec((tm, d), lambda i, j: (i, 0)),
            pl.BlockSpec((1, d), lambda i, j: (0, 0)),
            pl.BlockSpec((d, tn), lambda i, j: (0, j)),
            pl.BlockSpec((d, LANES), lambda i, j: (0, (j + 1) * (tn // LANES))),
            pl.BlockSpec((GATE_W, d), lambda i, j: (0, 0)),
        ],
        out_specs=[
            pl.BlockSpec((tm, tn), lambda i, j: (i, j)),
            pl.BlockSpec((GATE_W, tm), lambda i, j: (0, i)),
        ],
        out_shape=[jax.ShapeDtypeStruct((m, D_PROJ), F32), jax.ShapeDtypeStruct((GATE_W, m), F32)],
        scratch_shapes=[pltpu.VMEM((tm, d), BF16)],
        compiler_params=pltpu.CompilerParams(
            dimension_semantics=("parallel", "arbitrary"), vmem_limit_bytes=VMEM_LIMIT),
        name="in_proj",
    )(x, nw, w, w, wgt)


def _chunk_iotas():
    row = lax.broadcasted_iota(jnp.int32, (CHUNK, CHUNK), 0)
    col = lax.broadcasted_iota(jnp.int32, (CHUNK, CHUNK), 1)
    return row, col


def _gate_orientations(rows, row, col):
    tril = (row >= col).astype(F32)
    eye = (row == col).astype(F32)
    triu = (row <= col).astype(F32)
    both = _dot_nt(jnp.concatenate([tril, eye], axis=0), rows, precision=lax.Precision.HIGHEST)
    return _dot_hi(rows, triu), both[:CHUNK], both[CHUNK:]


def _lanes(x, c):
    return x[:, c * CHUNK:(c + 1) * CHUNK]


def _unit_lower_inverse(m, row, col, between):
    strict = row > col
    eye = (row == col).astype(F32)
    first = strict & (jnp.right_shift(row, 1) == jnp.right_shift(col, 1))
    t_inv = [eye - jnp.where(first, mi, 0.0) for mi in m]
    s = 1
    while (2 << s) <= CHUNK:
        join = (strict & (jnp.right_shift(row, s + 1) == jnp.right_shift(col, s + 1))
                & (jnp.right_shift(row, s) != jnp.right_shift(col, s)))
        cb = [jnp.where(join, mi, 0.0).astype(BF16) for mi in m]
        tb = [ti.astype(BF16) for ti in t_inv]
        x = [_dot(ci, ti).astype(BF16) for ci, ti in zip(cb, tb)]
        for fn in between.get(s, ()):
            fn()
        t_inv = [ti - _dot(tbi, xi) for ti, tbi, xi in zip(t_inv, tb, x)]
        for fn in between.get(s + 0.5, ()):
            fn()
        s += 1
    return t_inv


def _mixer_body(src_ref, dst_ref, ext_ref, z_ref, mq_ref, mk_ref, mv_ref, mo_ref, gr_ref, cw_ref,
                pcol_ref, nw_ref, y_ref, s_ref, c_ref, n_ref, m_ref):
    row, col = _chunk_iotas()
    causal = row >= col
    strict = row > col

    def chunk_rows(ref, c, c0, width):
        return ref[c * CHUNK:(c + 1) * CHUNK, c0:c0 + width]

    def conv_tile(lt):
        cols = slice(lt * LANES, (lt + 1) * LANES)
        span = CONV_STRIDE * SUBLANES
        for r0 in range(0, TB, span):
            for v in range(CONV_STRIDE):
                acc = None
                for tap in range(CONV_WIDTH):
                    lo = SUBLANES - (CONV_WIDTH - 1) + tap + r0 + v
                    term = ext_ref[lt, pl.ds(lo, SUBLANES, stride=CONV_STRIDE), :] * cw_ref[tap:tap + 1, cols]
                    acc = term if acc is None else acc + term
                y = _silu(acc)
                if lt < 2 * GDN_HEADS:
                    scale = GDN_DK ** -0.5 if lt < GDN_HEADS else 1.0
                    y = y * (lax.rsqrt(jnp.sum(y * y, axis=-1, keepdims=True) + NORM_EPS) * scale)
                dst_ref[lt, pl.ds(r0 + v, SUBLANES, stride=CONV_STRIDE), :] = y

    conv_tiles = [functools.partial(conv_tile, lt) for lt in range(GDN_QKV_W // LANES)]

    def emit_conv(count):
        for _ in range(min(count, len(conv_tiles))):
            conv_tiles.pop(0)()

    g_row = (-jnp.exp(pcol_ref[GA0:GA0 + GDN_HEADS, 1:2])
             * _softplus(gr_ref[GA0:GA0 + GDN_HEADS, :] + pcol_ref[GA0:GA0 + GDN_HEADS, 0:1]))
    beta_row = jax.nn.sigmoid(gr_ref[GB0:GB0 + GDN_HEADS, :])
    g_gates = jnp.concatenate([g_row, beta_row], axis=0)
    pre = gr_ref[MI0:MI0 + 2 * ML_HEADS, :] + pcol_ref[MI0:MI0 + 2 * ML_HEADS, 0:1]
    capped = GATE_SOFTCAP * jnp.tanh(pre / GATE_SOFTCAP)
    is_i = lax.broadcasted_iota(jnp.int32, capped.shape, 0) < ML_HEADS
    m_gates = jnp.where(is_i, capped, -_softplus(-capped))
    g_go = [_gate_orientations(_lanes(g_gates, c), row, col) for c in range(NCB)]
    m_go = [_gate_orientations(_lanes(m_gates, c), row, col) for c in range(NCB)]


    heads = range(GDN_HEADS)
    probs = [(c, h) for c in range(NCB) for h in heads]
    q = [src_ref[h, c * CHUNK:(c + 1) * CHUNK, :] for c, h in probs]
    k = [src_ref[GDN_HEADS + h, c * CHUNK:(c + 1) * CHUNK, :] for c, h in probs]
    vb = [src_ref[2 * GDN_HEADS + h, c * CHUNK:(c + 1) * CHUNK, :].astype(BF16) for c, h in probs]
    kb = [ki.astype(BF16) for ki in k]
    emit_conv(3)
    kt = [ki.T for ki in k]
    emit_conv(3)
    gc_r = [g_go[c][0][h:h + 1, :] for c, h in probs]
    gc_c = [g_go[c][1][:, h:h + 1] for c, h in probs]
    beta_c = [g_go[c][2][:, GB0 + h:GB0 + h + 1] for c, h in probs]
    beta_r = [_lanes(beta_row, c)[h:h + 1, :] for c, h in probs]
    decay = [jnp.exp(jnp.where(causal, a - b, NEG)) for a, b in zip(gc_c, gc_r)]
    kq = [_dot(jnp.concatenate([kbi, qi.astype(BF16)], axis=0), kti.astype(BF16))
          for kbi, qi, kti in zip(kb, q, kt)]
    emit_conv(3)
    m = [jnp.where(strict, x[:CHUNK] * d * b, 0.0) for x, d, b in zip(kq, decay, beta_c)]
    attn = [(x[CHUNK:] * d).astype(BF16) for x, d in zip(kq, decay)]

    mheads = range(ML_HEADS)
    mprobs = [(c, h) for c in range(NCB) for h in mheads]
    mq = [chunk_rows(mq_ref, c, h * ML_DQK, ML_DQK) * (ML_DQK ** -0.5) for c, h in mprobs]
    mk = [chunk_rows(mk_ref, c, h * ML_DQK, ML_DQK) for c, h in mprobs]
    mkt = [x.T for x in mk]
    emit_conv(3)
    mqb = [x.astype(BF16) for x in mq]
    mvb = [chunk_rows(mv_ref, c, h * ML_DV, ML_DV).astype(BF16) for c, h in mprobs]
    b_r = [m_go[c][0][ML_HEADS + h:ML_HEADS + h + 1, :] for c, h in mprobs]
    b_c = [m_go[c][1][:, ML_HEADS + h:ML_HEADS + h + 1] for c, h in mprobs]
    i_c = [m_go[c][2][:, h:h + 1] for c, h in mprobs]
    i_r = [_lanes(m_gates, c)[h:h + 1, :] for c, h in mprobs]
    b_end = [x[:, CHUNK - 1:CHUNK] for x in b_r]
    dmat = [jnp.where(causal, bc - br + ir, NEG) for bc, br, ir in zip(b_c, b_r, i_r)]
    m_intra = [jnp.max(d, axis=-1, keepdims=True) for d in dmat]
    mqk = [_dot(a, b.astype(BF16)) for a, b in zip(mqb, mkt)]

    ml_state = {
        "c": [c_ref[h] for h in mheads],
        "n": [n_ref[h:h + 1, :] for h in mheads],
        "m": [m_ref[h:h + 1, 0:1] for h in mheads],
    }

    def mlstm_chunk(c):
        c_st, n_st, m_st = ml_state["c"], ml_state["n"], ml_state["m"]
        ids = [c * ML_HEADS + h for h in mheads]
        inter_log = [b_c[i] + m_st[h] for h, i in zip(mheads, ids)]
        m_t = [jnp.maximum(a, m_intra[i]) for a, i in zip(inter_log, ids)]
        inter = [jnp.exp(a - b) for a, b in zip(inter_log, m_t)]
        p = [jnp.exp(dmat[i] - mt) * mqk[i] for mt, i in zip(m_t, ids)]
        num = [it * _dot(mqb[i], c_st[h].astype(BF16)) + _dot(pi.astype(BF16), mvb[i])
               for h, i, it, pi in zip(mheads, ids, inter, p)]
        den = [it * jnp.sum(mq[i] * n_st[h], axis=-1, keepdims=True) + jnp.sum(pi, axis=-1, keepdims=True)
               for h, i, it, pi in zip(mheads, ids, inter, p)]
        hout = [nu / jnp.maximum(jnp.abs(de), jnp.exp(-mt)) for nu, de, mt in zip(num, den, m_t)]
        m_new = [mt[CHUNK - 1:CHUNK, :] for mt in m_t]
        carry = [jnp.exp(b_end[i] + m_st[h] - mn) for h, i, mn in zip(mheads, ids, m_new)]
        ktw = [(mkt[i] * jnp.exp(b_end[i] - b_r[i] + i_r[i] - mn)).astype(BF16) for i, mn in zip(ids, m_new)]
        kw = [mk[i] * jnp.exp(b_end[i] - b_c[i] + i_c[i] - mn) for i, mn in zip(ids, m_new)]
        ml_state["c"] = [ca * c_st[h] + _dot(x, mvb[i]) for h, i, ca, x in zip(mheads, ids, carry, ktw)]
        ml_state["n"] = [ca * n_st[h] + jnp.sum(x, axis=0, keepdims=True) for h, ca, x in zip(mheads, carry, kw)]
        ml_state["m"] = m_new
        for h, ho in zip(mheads, hout):
            og = chunk_rows(mo_ref, c, h * ML_DV, ML_DV)
            y_ref[c * CHUNK:(c + 1) * CHUNK, GDN_V_W + h * ML_DV:GDN_V_W + (h + 1) * ML_DV] = (
                jax.nn.sigmoid(og) * ho).astype(y_ref.dtype)

    between = {}
    slots = [1, 1.5, 2, 2.5, 3, 3.5, 4, 4.5, 5, 5.5]
    for n in range(len(conv_tiles)):
        between.setdefault(slots[n % len(slots)], []).append(functools.partial(emit_conv, 1))
    for c in range(NCB):
        between.setdefault(2 + c, []).append(functools.partial(mlstm_chunk, c))
    t_inv = _unit_lower_inverse(m, row, col, between)
    for h in mheads:
        c_ref[h] = ml_state["c"][h]
        n_ref[h:h + 1, :] = ml_state["n"][h]
        m_ref[h:h + 1, :] = jnp.broadcast_to(ml_state["m"][h], (1, LANES))

    tbeta = [ti * b for ti, b in zip(t_inv, beta_r)]
    u = [_dot(x.astype(BF16), vi) for x, vi in zip(tbeta, vb)]
    w = [_dot((x * jnp.exp(g)).astype(BF16), kbi) for x, g, kbi in zip(tbeta, gc_r, kb)]
    gc_last = [g[:, CHUNK - 1:CHUNK] for g in gc_r]
    kt_dec = [(kti * jnp.exp(gl - g)).astype(BF16) for kti, gl, g in zip(kt, gc_last, gc_r)]
    g_end = [jnp.exp(gl) for gl in gc_last]
    wq_lhs = [jnp.concatenate([wi, qi * jnp.exp(g)], axis=0).astype(BF16) for wi, qi, g in zip(w, q, gc_c)]

    st = [s_ref[h] for h in heads]
    o = []
    for c in range(NCB):
        i0 = c * GDN_HEADS
        wq = [_dot(wq_lhs[i0 + h], st[h].astype(BF16)) for h in heads]
        vn = [(u[i0 + h] - wq[h][:CHUNK]).astype(BF16) for h in heads]
        o += [wq[h][CHUNK:] + _dot(attn[i0 + h], vn[h]) for h in heads]
        st = [g_end[i0 + h] * st[h] + _dot(kt_dec[i0 + h], vn[h]) for h in heads]
    for h in heads:
        s_ref[h] = st[h]

    nw = nw_ref[...]
    for (c, h), oi in zip(probs, o):
        oi = oi * lax.rsqrt(jnp.mean(oi * oi, axis=-1, keepdims=True) + NORM_EPS) * nw
        oi = oi * _silu(chunk_rows(z_ref, c, h * GDN_DV, GDN_DV))
        y_ref[c * CHUNK:(c + 1) * CHUNK, h * GDN_DV:(h + 1) * GDN_DV] = oi.astype(y_ref.dtype)


def _mixer_kernel(raw_ref, z_ref, mq_ref, mk_ref, mv_ref, mo_ref, gr_ref, cw_ref, pcol_ref, nw_ref,
                  y_ref, ext_ref, buf0_ref, buf1_ref, s_ref, c_ref, n_ref, m_ref):
    t = pl.program_id(1)

    @pl.when(t == 0)
    def _():
        ext_ref[:, 0:SUBLANES, :] = jnp.zeros((ext_ref.shape[0], SUBLANES, LANES), F32)
        buf1_ref[...] = jnp.zeros(buf1_ref.shape, F32)

    @pl.when(t <= 1)
    def _():
        s_ref[...] = jnp.zeros(s_ref.shape, F32)
        c_ref[...] = jnp.zeros(c_ref.shape, F32)
        n_ref[...] = jnp.zeros(n_ref.shape, F32)
        m_ref[...] = jnp.zeros(m_ref.shape, F32)

    n_tiles = ext_ref.shape[0]
    for lt in range(n_tiles):
        ext_ref[lt, SUBLANES:SUBLANES + TB, :] = raw_ref[:, lt * LANES:(lt + 1) * LANES]
    rest = (ext_ref, z_ref, mq_ref, mk_ref, mv_ref, mo_ref, gr_ref, cw_ref, pcol_ref, nw_ref,
            y_ref, s_ref, c_ref, n_ref, m_ref)

    @pl.when(t % 2 == 0)
    def _():
        _mixer_body(buf1_ref, buf0_ref, *rest)

    @pl.when(t % 2 == 1)
    def _():
        _mixer_body(buf0_ref, buf1_ref, *rest)

    ext_ref[:, 0:SUBLANES, :] = ext_ref[:, TB:TB + SUBLANES, :]


def _mixer(proj, gt, conv_w, pcol, norm_w, *, batch, seq):
    nt = seq // TB

    def nxt(b, t):
        return b * nt + jnp.minimum(t, nt - 1)

    def cur(b, t):
        return b * nt + jnp.maximum(t - 1, 0)

    return pl.pallas_call(
        _mixer_kernel,
        grid=(batch, nt + 1),
        in_specs=[
            pl.BlockSpec((TB, GDN_QKV_W), lambda b, t: (nxt(b, t), 0)),
            pl.BlockSpec((TB, GDN_V_W), lambda b, t: (cur(b, t), C_GZ // GDN_V_W)),
            pl.BlockSpec((TB, ML_QK_W), lambda b, t: (cur(b, t), C_MQ // ML_QK_W)),
            pl.BlockSpec((TB, ML_QK_W), lambda b, t: (cur(b, t), C_MK // ML_QK_W)),
            pl.BlockSpec((TB, ML_V_W), lambda b, t: (cur(b, t), C_MV // ML_V_W)),
            pl.BlockSpec((TB, ML_V_W), lambda b, t: (cur(b, t), C_MO // ML_V_W)),
            pl.BlockSpec((GATE_W, TB), lambda b, t: (0, cur(b, t))),
            pl.BlockSpec((CONV_WIDTH, GDN_QKV_W), lambda b, t: (0, 0)),
            pl.BlockSpec((GATE_W, LANES), lambda b, t: (0, 0)),
            pl.BlockSpec((1, GDN_DV), lambda b, t: (0, 0)),
        ],
        out_specs=pl.BlockSpec((TB, D_MIX), lambda b, t: (cur(b, t), 0)),
        out_shape=jax.ShapeDtypeStruct((batch * seq, D_MIX), BF16),
        scratch_shapes=[
            pltpu.VMEM((GDN_QKV_W // LANES, TB + SUBLANES, LANES), F32),
            pltpu.VMEM((GDN_QKV_W // LANES, TB, LANES), F32),
            pltpu.VMEM((GDN_QKV_W // LANES, TB, LANES), F32),
            pltpu.VMEM((GDN_HEADS, GDN_DK, GDN_DV), F32),
            pltpu.VMEM((ML_HEADS, ML_DQK, ML_DV), F32),
            pltpu.VMEM((SUBLANES, ML_DQK), F32),
            pltpu.VMEM((SUBLANES, LANES), F32),
        ],
        compiler_params=pltpu.CompilerParams(
            dimension_semantics=("parallel", "arbitrary"), vmem_limit_bytes=VMEM_LIMIT),
        name="mixer",
    )(proj, proj, proj, proj, proj, proj, gt, conv_w, pcol, norm_w)


def _out_proj_kernel(x_ref, y_ref, w_ref, o_ref):
    o_ref[...] = x_ref[...] + _dot(y_ref[...], w_ref[...])


def _out_proj(x, y, w, *, tm=512):
    m, d = x.shape
    return pl.pallas_call(
        _out_proj_kernel,
        grid=(m // tm,),
        in_specs=[
            pl.BlockSpec((tm, d), lambda i: (i, 0)),
            pl.BlockSpec((tm, D_MIX), lambda i: (i, 0)),
            pl.BlockSpec((D_MIX, d), lambda i: (0, 0)),
        ],
        out_specs=pl.BlockSpec((tm, d), lambda i: (i, 0)),
        out_shape=jax.ShapeDtypeStruct((m, d), F32),
        compiler_params=pltpu.CompilerParams(
            dimension_semantics=("parallel",), vmem_limit_bytes=VMEM_LIMIT),
        name="out_proj",
    )(x, y, w)


def _pad_rows(v, offset):
    return jnp.zeros((GATE_W,), F32).at[offset:offset + v.shape[0]].set(v.astype(F32))


def kernel(x, ffn1_norm_w, ffn1_w_gate, ffn1_w_up, ffn1_w_down, mix_norm_w, w_in, conv_w, gdn_a_log, gdn_dt_bias, gdn_norm_w, ml_i_bias, ml_f_bias, w_out, ffn2_norm_w, ffn2_w_gate, ffn2_w_up, ffn2_w_down, final_norm_w):
    batch, seq, d = x.shape
    depth = ffn1_norm_w.shape[0]
    h = x.reshape(batch * seq, d)
    fw = final_norm_w.reshape(1, d).astype(F32)
    m_lo = GDN_W + 2 * GDN_HEADS
    for l in range(depth):
        d_in = w_in.shape[2]
        w_proj = jnp.pad(w_in[l], ((0, 0), (0, -d_in % LANES))).astype(BF16)
        gates = jnp.concatenate([w_proj[:, GDN_W:m_lo], w_proj[:, m_lo + ML_W:d_in]], axis=1)
        wgt = jnp.zeros((GATE_W, d), BF16).at[:N_GATE].set(gates.T)
        bias = _pad_rows(gdn_dt_bias[l], GA0) + _pad_rows(ml_i_bias[l], MI0) + _pad_rows(ml_f_bias[l], MF0)
        alog = _pad_rows(gdn_a_log[l], GA0)
        pcol = jnp.zeros((GATE_W, LANES), F32).at[:, 0].set(bias).at[:, 1].set(alog)

        h = _ffn(h, ffn1_norm_w[l].reshape(1, d), ffn1_w_gate, ffn1_w_up, ffn1_w_down, fw,
                 layer=l, final_norm=False)
        proj, gt = _in_proj(h, mix_norm_w[l].reshape(1, d), w_proj, wgt)
        y = _mixer(proj, gt, conv_w[l].astype(F32), pcol, gdn_norm_w[l].reshape(1, GDN_DV).astype(F32),
                   batch=batch, seq=seq)
        h = _out_proj(h, y, w_out[l].astype(BF16))
        h = _ffn(h, ffn2_norm_w[l].reshape(1, d), ffn2_w_gate, ffn2_w_up, ffn2_w_down, fw,
                 layer=l, final_norm=(l == depth - 1))
    return h.reshape(batch, seq, d)
```

```python
import functools

import jax
import jax.numpy as jnp
from jax import lax
from jax.experimental import pallas as pl
from jax.experimental.pallas import tpu as pltpu

F32 = jnp.float32
BF16 = jnp.bfloat16

D_MODEL = 2048
D_FF = 5632
GDN_HEADS = 8
GDN_DK = 128
GDN_DV = 128
ML_HEADS = 4
ML_DQK = 128
ML_DV = 256
GDN_QK_W = GDN_HEADS * GDN_DK
GDN_V_W = GDN_HEADS * GDN_DV
GDN_QKV_W = 2 * GDN_QK_W + GDN_V_W
ML_QK_W = ML_HEADS * ML_DQK
ML_V_W = ML_HEADS * ML_DV
D_MIX = GDN_V_W + ML_V_W
CONV_WIDTH = 4
CHUNK = 64
NORM_EPS = 1e-6
GATE_SOFTCAP = 15.0

LANES = 128
SUBLANES = 8
GATE_W = LANES
GA0, GB0, MI0, MF0 = 0, GDN_HEADS, 2 * GDN_HEADS, 2 * GDN_HEADS + ML_HEADS
N_GATE = 2 * GDN_HEADS + 2 * ML_HEADS
GDN_W = GDN_QKV_W + GDN_V_W
ML_W = 2 * ML_QK_W + 2 * ML_V_W
C_GZ = GDN_QKV_W
C_MQ = GDN_W
C_MK = C_MQ + ML_QK_W
C_MV = C_MK + ML_QK_W
C_MO = C_MV + ML_V_W
D_PROJ = GDN_W + ML_W

TB = 2 * CHUNK
NCB = TB // CHUNK
CONV_STRIDE = 4
NEG = -1e30
VMEM_LIMIT = 60 * 1024 * 1024


def _rms(x, w):
    return x * lax.rsqrt(jnp.mean(x * x, axis=-1, keepdims=True) + NORM_EPS) * w


def _softplus(x):
    return jnp.maximum(x, 0.0) + jnp.log1p(jnp.exp(-jnp.abs(x)))


def _silu(x):
    return x * jax.nn.sigmoid(x)


def _dot(a, b):
    return jnp.dot(a, b, preferred_element_type=F32)


def _dot_nt(a, b, precision=None):
    return lax.dot_general(a, b, (((1,), (1,)), ((), ())), preferred_element_type=F32, precision=precision)


def _dot_hi(a, b):
    return jnp.dot(a, b, preferred_element_type=F32, precision=lax.Precision.HIGHEST)


def _ffn_kernel(x_ref, nw_ref, wg_hbm, wu_hbm, wd_hbm, fw_ref, o_ref,
                xn_ref, wg_buf, wu_buf, wd_buf, sem, *, layer, nf, tf, final_norm):
    i = pl.program_id(0)
    last_i = pl.num_programs(0) - 1

    def copies(f, slot):
        cols = pl.ds(pl.multiple_of(f * tf, tf), tf)
        return (pltpu.make_async_copy(wg_hbm.at[layer, :, cols], wg_buf.at[slot], sem.at[0, slot]),
                pltpu.make_async_copy(wu_hbm.at[layer, :, cols], wu_buf.at[slot], sem.at[1, slot]),
                pltpu.make_async_copy(wd_hbm.at[layer, cols, :], wd_buf.at[slot], sem.at[2, slot]))

    def start(f, slot):
        for c in copies(f, slot):
            c.start()

    def wait(f, slot):
        for c in copies(f, slot):
            c.wait()

    @pl.when(i == 0)
    def _():
        start(0, 0)

    xn_ref[...] = _rms(x_ref[...], nw_ref[...]).astype(BF16)
    o_ref[...] = jnp.zeros(o_ref.shape, F32)

    def tile(slot):
        wgu = jnp.concatenate([wg_buf[slot].astype(BF16), wu_buf[slot].astype(BF16)], axis=1)
        gu = _dot(xn_ref[...], wgu)
        a = (_silu(gu[:, :tf]) * gu[:, tf:]).astype(BF16)
        o_ref[...] += _dot(a, wd_buf[slot].astype(BF16))

    def pair(g, carry):
        f = 2 * g
        wait(f, 0)
        start(f + 1, 1)
        tile(0)
        wait(f + 1, 1)
        start(lax.rem(f + 2, nf), 0)
        tile(1)
        return carry

    lax.fori_loop(0, nf // 2, pair, 0)

    @pl.when(i == last_i)
    def _():
        wait(0, 0)

    y = x_ref[...] + 0.5 * o_ref[...]
    if final_norm:
        y = _rms(y, fw_ref[...])
    o_ref[...] = y


def _ffn(x, nw, wg, wu, wd, fw, *, layer, final_norm, tm=1024, tf=256):
    m, d = x.shape
    dff = wg.shape[2]
    nf = dff // tf
    assert nf % 2 == 0
    return pl.pallas_call(
        functools.partial(_ffn_kernel, layer=layer, nf=nf, tf=tf, final_norm=final_norm),
        grid=(m // tm,),
        in_specs=[
            pl.BlockSpec((tm, d), lambda i: (i, 0)),
            pl.BlockSpec((1, d), lambda i: (0, 0)),
            pl.BlockSpec(memory_space=pl.ANY),
            pl.BlockSpec(memory_space=pl.ANY),
            pl.BlockSpec(memory_space=pl.ANY),
            pl.BlockSpec((1, d), lambda i: (0, 0)),
        ],
        out_specs=pl.BlockSpec((tm, d), lambda i: (i, 0)),
        out_shape=jax.ShapeDtypeStruct((m, d), F32),
        scratch_shapes=[
            pltpu.VMEM((tm, d), BF16),
            pltpu.VMEM((2, d, tf), F32),
            pltpu.VMEM((2, d, tf), F32),
            pltpu.VMEM((2, tf, d), F32),
            pltpu.SemaphoreType.DMA((3, 2)),
        ],
        compiler_params=pltpu.CompilerParams(
            dimension_semantics=("arbitrary",), vmem_limit_bytes=VMEM_LIMIT),
        name="ffn",
    )(x, nw, wg, wu, wd, fw)


def _in_proj_kernel(x_ref, nw_ref, w_ref, wx_ref, wgt_ref, p_ref, gt_ref, xn_ref, *, na, shift):
    j = pl.program_id(1)

    @pl.when(j == 0)
    def _():
        xn = _rms(x_ref[...], nw_ref[...]).astype(BF16)
        xn_ref[...] = xn
        gt_ref[...] = _dot_nt(wgt_ref[...], xn)

    @pl.when(j < na)
    def _():
        p_ref[...] = _dot(xn_ref[...], w_ref[...])

    @pl.when(j >= na)
    def _():
        tn = w_ref.shape[1]
        w = jnp.concatenate([w_ref[...], wx_ref[...]], axis=1)
        w = pltpu.bitcast(pltpu.bitcast(w, jnp.uint32)[:, shift:shift + tn], BF16)
        p_ref[...] = _dot(xn_ref[...], w)


def _in_proj(x, nw, w, wgt, *, tm=1024, tn=1024):
    m, d = x.shape
    na = GDN_W // tn
    nb = ML_W // tn
    return pl.pallas_call(
        functools.partial(_in_proj_kernel, na=na, shift=2 * GDN_HEADS),
        grid=(m // tm, na + nb),
        in_specs=[
            pl.BlockSpec((tm, d), lambda i, j: (i, 0)),
            pl.BlockSpec((1, d), lambda i, j: (0, 0)),
            pl.BlockSpec((d, tn), lambda i, j: (0, j)),
            pl.BlockSpec((d, LANES), lambda i, j: (0, (j + 1) * (tn // LANES))),
            pl.BlockSpec((GATE_W, d), lambda i, j: (0, 0)),
        ],
        out_specs=[
            pl.BlockSpec((tm, tn), lambda i, j: (i, j)),
            pl.BlockSpec((GATE_W, tm), lambda i, j: (0, i)),
        ],
        out_shape=[jax.ShapeDtypeStruct((m, D_PROJ), F32), jax.ShapeDtypeStruct((GATE_W, m), F32)],
        scratch_shapes=[pltpu.VMEM((tm, d), BF16)],
        compiler_params=pltpu.CompilerParams(
            dimension_semantics=("parallel", "arbitrary"), vmem_limit_bytes=VMEM_LIMIT),
        name="in_proj",
    )(x, nw, w, w, wgt)


def _chunk_iotas():
    row = lax.broadcasted_iota(jnp.int32, (CHUNK, CHUNK), 0)
    col = lax.broadcasted_iota(jnp.int32, (CHUNK, CHUNK), 1)
    return row, col


def _gate_orientations(rows, row, col):
    tril = (row >= col).astype(F32)
    eye = (row == col).astype(F32)
    triu = (row <= col).astype(F32)
    both = _dot_nt(jnp.concatenate([tril, eye], axis=0), rows, precision=lax.Precision.HIGHEST)
    return _dot_hi(rows, triu), both[:CHUNK], both[CHUNK:]


def _lanes(x, c):
    return x[:, c * CHUNK:(c + 1) * CHUNK]


def _unit_lower_inverse(m, row, col, between):
    strict = row > col
    eye = (row == col).astype(F32)
    first = strict & (jnp.right_shift(row, 1) == jnp.right_shift(col, 1))
    t_inv = [eye - jnp.where(first, mi, 0.0) for mi in m]
    s = 1
    while (2 << s) <= CHUNK:
        join = (strict & (jnp.right_shift(row, s + 1) == jnp.right_shift(col, s + 1))
                & (jnp.right_shift(row, s) != jnp.right_shift(col, s)))
        cb = [jnp.where(join, mi, 0.0).astype(BF16) for mi in m]
        tb = [ti.astype(BF16) for ti in t_inv]
        x = [_dot(ci, ti).astype(BF16) for ci, ti in zip(cb, tb)]
        for fn in between.get(s, ()):
            fn()
        t_inv = [ti - _dot(tbi, xi) for ti, tbi, xi in zip(t_inv, tb, x)]
        for fn in between.get(s + 0.5, ()):
            fn()
        s += 1
    return t_inv


def _mixer_body(src_ref, dst_ref, ext_ref, z_ref, mq_ref, mk_ref, mv_ref, mo_ref, gr_ref, cw_ref,
                pcol_ref, nw_ref, y_ref, s_ref, c_ref, n_ref, m_ref):
    row, col = _chunk_iotas()
    causal = row >= col
    strict = row > col

    def chunk_rows(ref, c, c0, width):
        return ref[c * CHUNK:(c + 1) * CHUNK, c0:c0 + width]

    def conv_tile(lt):
        cols = slice(lt * LANES, (lt + 1) * LANES)
        span = CONV_STRIDE * SUBLANES
        for r0 in range(0, TB, span):
            for v in range(CONV_STRIDE):
                acc = None
                for tap in range(CONV_WIDTH):
                    lo = SUBLANES - (CONV_WIDTH - 1) + tap + r0 + v
                    term = ext_ref[lt, pl.ds(lo, SUBLANES, stride=CONV_STRIDE), :] * cw_ref[tap:tap + 1, cols]
                    acc = term if acc is None else acc + term
                y = _silu(acc)
                if lt < 2 * GDN_HEADS:
                    scale = GDN_DK ** -0.5 if lt < GDN_HEADS else 1.0
                    y = y * (lax.rsqrt(jnp.sum(y * y, axis=-1, keepdims=True) + NORM_EPS) * scale)
                dst_ref[lt, pl.ds(r0 + v, SUBLANES, stride=CONV_STRIDE), :] = y

    conv_tiles = [functools.partial(conv_tile, lt) for lt in range(GDN_QKV_W // LANES)]

    def emit_conv(count):
        for _ in range(min(count, len(conv_tiles))):
            conv_tiles.pop(0)()

    g_row = (-jnp.exp(pcol_ref[GA0:GA0 + GDN_HEADS, 1:2])
             * _softplus(gr_ref[GA0:GA0 + GDN_HEADS, :] + pcol_ref[GA0:GA0 + GDN_HEADS, 0:1]))
    beta_row = jax.nn.sigmoid(gr_ref[GB0:GB0 + GDN_HEADS, :])
    g_gates = jnp.concatenate([g_row, beta_row], axis=0)
    pre = gr_ref[MI0:MI0 + 2 * ML_HEADS, :] + pcol_ref[MI0:MI0 + 2 * ML_HEADS, 0:1]
    capped = GATE_SOFTCAP * jnp.tanh(pre / GATE_SOFTCAP)
    is_i = lax.broadcasted_iota(jnp.int32, capped.shape, 0) < ML_HEADS
    m_gates = jnp.where(is_i, capped, -_softplus(-capped))
    g_go = [_gate_orientations(_lanes(g_gates, c), row, col) for c in range(NCB)]
    m_go = [_gate_orientations(_lanes(m_gates, c), row, col) for c in range(NCB)]


    heads = range(GDN_HEADS)
    probs = [(c, h) for c in range(NCB) for h in heads]
    q = [src_ref[h, c * CHUNK:(c + 1) * CHUNK, :] for c, h in probs]
    k = [src_ref[GDN_HEADS + h, c * CHUNK:(c + 1) * CHUNK, :] for c, h in probs]
    vb = [src_ref[2 * GDN_HEADS + h, c * CHUNK:(c + 1) * CHUNK, :].astype(BF16) for c, h in probs]
    kb = [ki.astype(BF16) for ki in k]
    emit_conv(3)
    kt = [ki.T for ki in k]
    emit_conv(3)
    gc_r = [g_go[c][0][h:h + 1, :] for c, h in probs]
    gc_c = [g_go[c][1][:, h:h + 1] for c, h in probs]
    beta_c = [g_go[c][2][:, GB0 + h:GB0 + h + 1] for c, h in probs]
    beta_r = [_lanes(beta_row, c)[h:h + 1, :] for c, h in probs]
    decay = [jnp.exp(jnp.where(causal, a - b, NEG)) for a, b in zip(gc_c, gc_r)]
    kq = [_dot(jnp.concatenate([kbi, qi.astype(BF16)], axis=0), kti.astype(BF16))
          for kbi, qi, kti in zip(kb, q, kt)]
    emit_conv(3)
    m = [jnp.where(strict, x[:CHUNK] * d * b, 0.0) for x, d, b in zip(kq, decay, beta_c)]
    attn = [(x[CHUNK:] * d).astype(BF16) for x, d in zip(kq, decay)]

    mheads = range(ML_HEADS)
    mprobs = [(c, h) for c in range(NCB) for h in mheads]
    mq = [chunk_rows(mq_ref, c, h * ML_DQK, ML_DQK) * (ML_DQK ** -0.5) for c, h in mprobs]
    mk = [chunk_rows(mk_ref, c, h * ML_DQK, ML_DQK) for c, h in mprobs]
    mkt = [x.T for x in mk]
    emit_conv(3)
    mqb = [x.astype(BF16) for x in mq]
    mvb = [chunk_rows(mv_ref, c, h * ML_DV, ML_DV).astype(BF16) for c, h in mprobs]
    b_r = [m_go[c][0][ML_HEADS + h:ML_HEADS + h + 1, :] for c, h in mprobs]
    b_c = [m_go[c][1][:, ML_HEADS + h:ML_HEADS + h + 1] for c, h in mprobs]
    i_c = [m_go[c][2][:, h:h + 1] for c, h in mprobs]
    i_r = [_lanes(m_gates, c)[h:h + 1, :] for c, h in mprobs]
    b_end = [x[:, CHUNK - 1:CHUNK] for x in b_r]
    dmat = [jnp.where(causal, bc - br + ir, NEG) for bc, br, ir in zip(b_c, b_r, i_r)]
    m_intra = [jnp.max(d, axis=-1, keepdims=True) for d in dmat]
    mqk = [_dot(a, b.astype(BF16)) for a, b in zip(mqb, mkt)]

    ml_state = {
        "c": [c_ref[h] for h in mheads],
        "n": [n_ref[h:h + 1, :] for h in mheads],
        "m": [m_ref[h:h + 1, 0:1] for h in mheads],
    }

    def mlstm_chunk(c):
        c_st, n_st, m_st = ml_state["c"], ml_state["n"], ml_state["m"]
        ids = [c * ML_HEADS + h for h in mheads]
        inter_log = [b_c[i] + m_st[h] for h, i in zip(mheads, ids)]
        m_t = [jnp.maximum(a, m_intra[i]) for a, i in zip(inter_log, ids)]
        inter = [jnp.exp(a - b) for a, b in zip(inter_log, m_t)]
        p = [jnp.exp(dmat[i] - mt) * mqk[i] for mt, i in zip(m_t, ids)]
        num = [it * _dot(mqb[i], c_st[h].astype(BF16)) + _dot(pi.astype(BF16), mvb[i])
               for h, i, it, pi in zip(mheads, ids, inter, p)]
        den = [it * jnp.sum(mq[i] * n_st[h], axis=-1, keepdims=True) + jnp.sum(pi, axis=-1, keepdims=True)
               for h, i, it, pi in zip(mheads, ids, inter, p)]
        hout = [nu / jnp.maximum(jnp.abs(de), jnp.exp(-mt)) for nu, de, mt in zip(num, den, m_t)]
        m_new = [mt[CHUNK - 1:CHUNK, :] for mt in m_t]
        carry = [jnp.exp(b_end[i] + m_st[h] - mn) for h, i, mn in zip(mheads, ids, m_new)]
        ktw = [(mkt[i] * jnp.exp(b_end[i] - b_r[i] + i_r[i] - mn)).astype(BF16) for i, mn in zip(ids, m_new)]
        kw = [mk[i] * jnp.exp(b_end[i] - b_c[i] + i_c[i] - mn) for i, mn in zip(ids, m_new)]
        ml_state["c"] = [ca * c_st[h] + _dot(x, mvb[i]) for h, i, ca, x in zip(mheads, ids, carry, ktw)]
        ml_state["n"] = [ca * n_st[h] + jnp.sum(x, axis=0, keepdims=True) for h, ca, x in zip(mheads, carry, kw)]
        ml_state["m"] = m_new
        for h, ho in zip(mheads, hout):
            og = chunk_rows(mo_ref, c, h * ML_DV, ML_DV)
            y_ref[c * CHUNK:(c + 1) * CHUNK, GDN_V_W + h * ML_DV:GDN_V_W + (h + 1) * ML_DV] = (
                jax.nn.sigmoid(og) * ho).astype(y_ref.dtype)

    between = {}
    slots = [1, 1.5, 2, 2.5, 3, 3.5, 4, 4.5, 5, 5.5]
    for n in range(len(conv_tiles)):
        between.setdefault(slots[n % len(slots)], []).append(functools.partial(emit_conv, 1))
    for c in range(NCB):
        between.setdefault(2 + c, []).append(functools.partial(mlstm_chunk, c))
    t_inv = _unit_lower_inverse(m, row, col, between)
    for h in mheads:
        c_ref[h] = ml_state["c"][h]
        n_ref[h:h + 1, :] = ml_state["n"][h]
        m_ref[h:h + 1, :] = jnp.broadcast_to(ml_state["m"][h], (1, LANES))

    tbeta = [ti * b for ti, b in zip(t_inv, beta_r)]
    u = [_dot(x.astype(BF16), vi) for x, vi in zip(tbeta, vb)]
    w = [_dot((x * jnp.exp(g)).astype(BF16), kbi) for x, g, kbi in zip(tbeta, gc_r, kb)]
    gc_last = [g[:, CHUNK - 1:CHUNK] for g in gc_r]
    kt_dec = [(kti * jnp.exp(gl - g)).astype(BF16) for kti, gl, g in zip(kt, gc_last, gc_r)]
    g_end = [jnp.exp(gl) for gl in gc_last]
    wq_lhs = [jnp.concatenate([wi, qi * jnp.exp(g)], axis=0).astype(BF16) for wi, qi, g in zip(w, q, gc_c)]

    st = [s_ref[h] for h in heads]
    o = []
    for c in range(NCB):
        i0 = c * GDN_HEADS
        wq = [_dot(wq_lhs[i0 + h], st[h].astype(BF16)) for h in heads]
        vn = [(u[i0 + h] - wq[h][:CHUNK]).astype(BF16) for h in heads]
        o += [wq[h][CHUNK:] + _dot(attn[i0 + h], vn[h]) for h in heads]
        st = [g_end[i0 + h] * st[h] + _dot(kt_dec[i0 + h], vn[h]) for h in heads]
    for h in heads:
        s_ref[h] = st[h]

    nw = nw_ref[...]
    for (c, h), oi in zip(probs, o):
        oi = oi * lax.rsqrt(jnp.mean(oi * oi, axis=-1, keepdims=True) + NORM_EPS) * nw
        oi = oi * _silu(chunk_rows(z_ref, c, h * GDN_DV, GDN_DV))
        y_ref[c * CHUNK:(c + 1) * CHUNK, h * GDN_DV:(h + 1) * GDN_DV] = oi.astype(y_ref.dtype)


def _mixer_kernel(raw_ref, z_ref, mq_ref, mk_ref, mv_ref, mo_ref, gr_ref, cw_ref, pcol_ref, nw_ref,
                  y_ref, ext_ref, buf0_ref, buf1_ref, s_ref, c_ref, n_ref, m_ref):
    t = pl.program_id(1)

    @pl.when(t == 0)
    def _():
        ext_ref[:, 0:SUBLANES, :] = jnp.zeros((ext_ref.shape[0], SUBLANES, LANES), F32)
        buf1_ref[...] = jnp.zeros(buf1_ref.shape, F32)

    @pl.when(t <= 1)
    def _():
        s_ref[...] = jnp.zeros(s_ref.shape, F32)
        c_ref[...] = jnp.zeros(c_ref.shape, F32)
        n_ref[...] = jnp.zeros(n_ref.shape, F32)
        m_ref[...] = jnp.zeros(m_ref.shape, F32)

    n_tiles = ext_ref.shape[0]
    for lt in range(n_tiles):
        ext_ref[lt, SUBLANES:SUBLANES + TB, :] = raw_ref[:, lt * LANES:(lt + 1) * LANES]
    rest = (ext_ref, z_ref, mq_ref, mk_ref, mv_ref, mo_ref, gr_ref, cw_ref, pcol_ref, nw_ref,
            y_ref, s_ref, c_ref, n_ref, m_ref)

    @pl.when(t % 2 == 0)
    def _():
        _mixer_body(buf1_ref, buf0_ref, *rest)

    @pl.when(t % 2 == 1)
    def _():
        _mixer_body(buf0_ref, buf1_ref, *rest)

    ext_ref[:, 0:SUBLANES, :] = ext_ref[:, TB:TB + SUBLANES, :]


def _mixer(proj, gt, conv_w, pcol, norm_w, *, batch, seq):
    nt = seq // TB

    def nxt(b, t):
        return b * nt + jnp.minimum(t, nt - 1)

    def cur(b, t):
        return b * nt + jnp.maximum(t - 1, 0)

    return pl.pallas_call(
        _mixer_kernel,
        grid=(batch, nt + 1),
        in_specs=[
            pl.BlockSpec((TB, GDN_QKV_W), lambda b, t: (nxt(b, t), 0)),
            pl.BlockSpec((TB, GDN_V_W), lambda b, t: (cur(b, t), C_GZ // GDN_V_W)),
            pl.BlockSpec((TB, ML_QK_W), lambda b, t: (cur(b, t), C_MQ // ML_QK_W)),
            pl.BlockSpec((TB, ML_QK_W), lambda b, t: (cur(b, t), C_MK // ML_QK_W)),
            pl.BlockSpec((TB, ML_V_W), lambda b, t: (cur(b, t), C_MV // ML_V_W)),
            pl.BlockSpec((TB, ML_V_W), lambda b, t: (cur(b, t), C_MO // ML_V_W)),
            pl.BlockSpec((GATE_W, TB), lambda b, t: (0, cur(b, t))),
            pl.BlockSpec((CONV_WIDTH, GDN_QKV_W), lambda b, t: (0, 0)),
            pl.BlockSpec((GATE_W, LANES), lambda b, t: (0, 0)),
            pl.BlockSpec((1, GDN_DV), lambda b, t: (0, 0)),
        ],
        out_specs=pl.BlockSpec((TB, D_MIX), lambda b, t: (cur(b, t), 0)),
        out_shape=jax.ShapeDtypeStruct((batch * seq, D_MIX), BF16),
        scratch_shapes=[
            pltpu.VMEM((GDN_QKV_W // LANES, TB + SUBLANES, LANES), F32),
            pltpu.VMEM((GDN_QKV_W // LANES, TB, LANES), F32),
            pltpu.VMEM((GDN_QKV_W // LANES, TB, LANES), F32),
            pltpu.VMEM((GDN_HEADS, GDN_DK, GDN_DV), F32),
            pltpu.VMEM((ML_HEADS, ML_DQK, ML_DV), F32),
            pltpu.VMEM((SUBLANES, ML_DQK), F32),
            pltpu.VMEM((SUBLANES, LANES), F32),
        ],
        compiler_params=pltpu.CompilerParams(
            dimension_semantics=("parallel", "arbitrary"), vmem_limit_bytes=VMEM_LIMIT),
        name="mixer",
    )(proj, proj, proj, proj, proj, proj, gt, conv_w, pcol, norm_w)


def _out_proj_kernel(x_ref, y_ref, w_ref, o_ref):
    o_ref[...] = x_ref[...] + _dot(y_ref[...], w_ref[...])


def _out_proj(x, y, w, *, tm=512):
    m, d = x.shape
    return pl.pallas_call(
        _out_proj_kernel,
        grid=(m // tm,),
        in_specs=[
            pl.BlockSpec((tm, d), lambda i: (i, 0)),
            pl.BlockSpec((tm, D_MIX), lambda i: (i, 0)),
            pl.BlockSpec((D_MIX, d), lambda i: (0, 0)),
        ],
        out_specs=pl.BlockSpec((tm, d), lambda i: (i, 0)),
        out_shape=jax.ShapeDtypeStruct((m, d), F32),
        compiler_params=pltpu.CompilerParams(
            dimension_semantics=("parallel",), vmem_limit_bytes=VMEM_LIMIT),
        name="out_proj",
    )(x, y, w)


def _pad_rows(v, offset):
    return jnp.zeros((GATE_W,), F32).at[offset:offset + v.shape[0]].set(v.astype(F32))


def kernel(x, ffn1_norm_w, ffn1_w_gate, ffn1_w_up, ffn1_w_down, mix_norm_w, w_in, conv_w, gdn_a_log, gdn_dt_bias, gdn_norm_w, ml_i_bias, ml_f_bias, w_out, ffn2_norm_w, ffn2_w_gate, ffn2_w_up, ffn2_w_down, final_norm_w):
    batch, seq, d = x.shape
    depth = ffn1_norm_w.shape[0]
    h = x.reshape(batch * seq, d)
    fw = final_norm_w.reshape(1, d).astype(F32)
    m_lo = GDN_W + 2 * GDN_HEADS
    for l in range(depth):
        d_in = w_in.shape[2]
        w_proj = jnp.pad(w_in[l], ((0, 0), (0, -d_in % LANES))).astype(BF16)
        gates = jnp.concatenate([w_proj[:, GDN_W:m_lo], w_proj[:, m_lo + ML_W:d_in]], axis=1)
        wgt = jnp.zeros((GATE_W, d), BF16).at[:N_GATE].set(gates.T)
        bias = _pad_rows(gdn_dt_bias[l], GA0) + _pad_rows(ml_i_bias[l], MI0) + _pad_rows(ml_f_bias[l], MF0)
        alog = _pad_rows(gdn_a_log[l], GA0)
        pcol = jnp.zeros((GATE_W, LANES), F32).at[:, 0].set(bias).at[:, 1].set(alog)

        h = _ffn(h, ffn1_norm_w[l].reshape(1, d), ffn1_w_gate, ffn1_w_up, ffn1_w_down, fw,
                 layer=l, final_norm=False)
        proj, gt = _in_proj(h, mix_norm_w[l].reshape(1, d), w_proj, wgt)
        y = _mixer(proj, gt, conv_w[l].astype(F32), pcol, gdn_norm_w[l].reshape(1, GDN_DV).astype(F32),
                   batch=batch, seq=seq)
        h = _out_proj(h, y, w_out[l].astype(BF16))
        h = _ffn(h, ffn2_norm_w[l].reshape(1, d), ffn2_w_gate, ffn2_w_up, ffn2_w_down, fw,
                 layer=l, final_norm=(l == depth - 1))
    return h.reshape(batch, seq, d)
```

```python
import functools

import jax
import jax.numpy as jnp
from jax import lax
from jax.experimental import pallas as pl
from jax.experimental.pallas import tpu as pltpu

F32 = jnp.float32
BF16 = jnp.bfloat16

D_MODEL = 2048
D_FF = 5632
GDN_HEADS = 8
GDN_DK = 128
GDN_DV = 128
ML_HEADS = 4
ML_DQK = 128
ML_DV = 256
GDN_QK_W = GDN_HEADS * GDN_DK
GDN_V_W = GDN_HEADS * GDN_DV
GDN_QKV_W = 2 * GDN_QK_W + GDN_V_W
ML_QK_W = ML_HEADS * ML_DQK
ML_V_W = ML_HEADS * ML_DV
D_MIX = GDN_V_W + ML_V_W
CONV_WIDTH = 4
CHUNK = 64
NORM_EPS = 1e-6
GATE_SOFTCAP = 15.0

LANES = 128
SUBLANES = 8
MXU_W = 256
GATE_W = LANES
GA0, GB0, MI0, MF0 = 0, GDN_HEADS, 2 * GDN_HEADS, 2 * GDN_HEADS + ML_HEADS
N_GATE = 2 * GDN_HEADS + 2 * ML_HEADS
GDN_W = GDN_QKV_W + GDN_V_W
ML_W = 2 * ML_QK_W + 2 * ML_V_W
C_GZ = GDN_QKV_W
C_MQ = GDN_W
C_MK = C_MQ + ML_QK_W
C_MV = C_MK + ML_QK_W
C_MO = C_MV + ML_V_W
D_PROJ = GDN_W + ML_W

TB = 2 * CHUNK
NCB = TB // CHUNK
CONV_STRIDE = 4
NEG = -1e30
VMEM_LIMIT = 60 * 1024 * 1024


def _rms(x, w):
    return x * lax.rsqrt(jnp.mean(x * x, axis=-1, keepdims=True) + NORM_EPS) * w


def _softplus(x):
    return jnp.maximum(x, 0.0) + jnp.log1p(jnp.exp(-jnp.abs(x)))


def _silu(x):
    return x * jax.nn.sigmoid(x)


def _dot(a, b):
    return jnp.dot(a, b, preferred_element_type=F32)


def _dot_nt(a, b, precision=None):
    return lax.dot_general(a, b, (((1,), (1,)), ((), ())), preferred_element_type=F32, precision=precision)


def _dot_hi(a, b):
    return jnp.dot(a, b, preferred_element_type=F32, precision=lax.Precision.HIGHEST)


def _ffn_kernel(x_ref, nw_ref, wg_hbm, wu_hbm, wd_hbm, fw_ref, o_ref,
                xn_ref, wg_buf, wu_buf, wd_buf, sem, *, layer, nf, tf, final_norm):
    i = pl.program_id(0)
    last_i = pl.num_programs(0) - 1

    def copies(f, slot):
        cols = pl.ds(pl.multiple_of(f * tf, tf), tf)
        return (pltpu.make_async_copy(wg_hbm.at[layer, :, cols], wg_buf.at[slot], sem.at[0, slot]),
                pltpu.make_async_copy(wu_hbm.at[layer, :, cols], wu_buf.at[slot], sem.at[1, slot]),
                pltpu.make_async_copy(wd_hbm.at[layer, cols, :], wd_buf.at[slot], sem.at[2, slot]))

    def start(f, slot):
        for c in copies(f, slot):
            c.start()

    def wait(f, slot):
        for c in copies(f, slot):
            c.wait()

    @pl.when(i == 0)
    def _():
        start(0, 0)

    xn_ref[...] = _rms(x_ref[...], nw_ref[...]).astype(BF16)
    o_ref[...] = jnp.zeros(o_ref.shape, F32)

    def tile(slot):
        wgu = jnp.concatenate([wg_buf[slot].astype(BF16), wu_buf[slot].astype(BF16)], axis=1)
        gu = _dot(xn_ref[...], wgu)
        a = (_silu(gu[:, :tf]) * gu[:, tf:]).astype(BF16)
        o_ref[...] += _dot(a, wd_buf[slot].astype(BF16))

    def pair(g, carry):
        f = 2 * g
        wait(f, 0)
        start(f + 1, 1)
        tile(0)
        wait(f + 1, 1)
        start(lax.rem(f + 2, nf), 0)
        tile(1)
        return carry

    lax.fori_loop(0, nf // 2, pair, 0)

    @pl.when(i == last_i)
    def _():
        wait(0, 0)

    y = x_ref[...] + 0.5 * o_ref[...]
    if final_norm:
        y = _rms(y, fw_ref[...])
    o_ref[...] = y


def _ffn(x, nw, wg, wu, wd, fw, *, layer, final_norm, tm=1024, tf=256):
    m, d = x.shape
    dff = wg.shape[2]
    nf = dff // tf
    assert nf % 2 == 0
    return pl.pallas_call(
        functools.partial(_ffn_kernel, layer=layer, nf=nf, tf=tf, final_norm=final_norm),
        grid=(m // tm,),
        in_specs=[
            pl.BlockSpec((tm, d), lambda i: (i, 0)),
            pl.BlockSpec((1, d), lambda i: (0, 0)),
            pl.BlockSpec(memory_space=pl.ANY),
            pl.BlockSpec(memory_space=pl.ANY),
            pl.BlockSpec(memory_space=pl.ANY),
            pl.BlockSpec((1, d), lambda i: (0, 0)),
        ],
        out_specs=pl.BlockSpec((tm, d), lambda i: (i, 0)),
        out_shape=jax.ShapeDtypeStruct((m, d), F32),
        scratch_shapes=[
            pltpu.VMEM((tm, d), BF16),
            pltpu.VMEM((2, d, tf), F32),
            pltpu.VMEM((2, d, tf), F32),
            pltpu.VMEM((2, tf, d), F32),
            pltpu.SemaphoreType.DMA((3, 2)),
        ],
        compiler_params=pltpu.CompilerParams(
            dimension_semantics=("arbitrary",), vmem_limit_bytes=VMEM_LIMIT),
        name="ffn",
    )(x, nw, wg, wu, wd, fw)


def _in_proj_kernel(x_ref, nw_ref, w_ref, wx_ref, wgt_ref, cw_ref, qkv_ref, rest_ref, gt_ref,
                    xn_ref, scr_a, scr_b, hist_ref, *, shift, tiles_per_seq):
    i = pl.program_id(0)
    j = pl.program_id(1)
    tm = x_ref.shape[0]
    tn = w_ref.shape[1]
    n_slab = tn // LANES

    @pl.when(j == 0)
    def _():
        xn = _rms(x_ref[...], nw_ref[...]).astype(BF16)
        xn_ref[...] = xn
        gt_ref[...] = _dot_nt(wgt_ref[...], xn)

    @pl.when((j == 0) & (i % tiles_per_seq == 0))
    def _():
        hist_ref[...] = jnp.zeros(hist_ref.shape, F32)

    n_mm = tn // MXU_W

    def to_scratch(scr):
        def store(n, p):
            for g in range(MXU_W // LANES):
                scr[n * (MXU_W // LANES) + g, SUBLANES:SUBLANES + tm, :] = p[:, g * LANES:(g + 1) * LANES]
        return store

    def to_rest(n, p):
        rest_ref[:, n * MXU_W:(n + 1) * MXU_W] = p

    def project(store, conv=None):
        for n in range(n_mm):
            store(n, _dot(xn_ref[...], w_ref[:, n * MXU_W:(n + 1) * MXU_W]))
            if conv is not None:
                scr, slot, l2_scale = conv
                conv_from(scr, slot, l2_scale, range(n * n_slab // n_mm, (n + 1) * n_slab // n_mm))

    def conv_from(scr, slot, l2_scale, slabs):
        span = CONV_STRIDE * SUBLANES
        for g in slabs:
            cols = slice(g * LANES, (g + 1) * LANES)
            scr[g, 0:SUBLANES, :] = hist_ref[slot, g]
            hist_ref[slot, g] = scr[g, tm:tm + SUBLANES, :]
            for r0 in range(0, tm, span):
                for v in range(CONV_STRIDE):
                    acc = None
                    for tap in range(CONV_WIDTH):
                        lo = SUBLANES - (CONV_WIDTH - 1) + tap + r0 + v
                        term = scr[g, pl.ds(lo, SUBLANES, stride=CONV_STRIDE), :] * cw_ref[tap:tap + 1, cols]
                        acc = term if acc is None else acc + term
                    y = _silu(acc)
                    if l2_scale is not None:
                        y = y * (lax.rsqrt(jnp.sum(y * y, axis=-1, keepdims=True) + NORM_EPS) * l2_scale)
                    qkv_ref[g, pl.ds(r0 + v, SUBLANES, stride=CONV_STRIDE), :] = y

    @pl.when(j == 0)
    def _():
        project(to_scratch(scr_a))

    @pl.when(j == 1)
    def _():
        project(to_scratch(scr_b), (scr_a, 0, GDN_DK ** -0.5))

    @pl.when(j == 2)
    def _():
        project(to_scratch(scr_a), (scr_b, 1, 1.0))

    @pl.when(j == 3)
    def _():
        project(to_rest, (scr_a, 2, None))

    @pl.when(j > 3)
    def _():
        w = jnp.concatenate([w_ref[...], wx_ref[...]], axis=1)
        w = pltpu.bitcast(pltpu.bitcast(w, jnp.uint32)[:, shift:shift + tn], BF16)
        rest_ref[...] = _dot(xn_ref[...], w)


def _in_proj(x, nw, w, wgt, conv_w, *, seq, tm=1024, tn=GDN_QK_W):
    m, d = x.shape
    assert tn == GDN_QK_W == GDN_V_W and seq % tm == 0
    n_conv = GDN_QKV_W // tn
    n_tiles = D_PROJ // tn
    n_slab = tn // LANES
    return pl.pallas_call(
        functools.partial(_in_proj_kernel, shift=2 * GDN_HEADS, tiles_per_seq=seq // tm),
        grid=(m // tm, n_tiles),
        in_specs=[
            pl.BlockSpec((tm, d), lambda i, j: (i, 0)),
            pl.BlockSpec((1, d), lambda i, j: (0, 0)),
            pl.BlockSpec((d, tn), lambda i, j: (0, j)),
            pl.BlockSpec((d, LANES), lambda i, j: (0, (j + 1) * n_slab)),
            pl.BlockSpec((GATE_W, d), lambda i, j: (0, 0)),
            pl.BlockSpec((CONV_WIDTH, tn), lambda i, j: (0, jnp.clip(j - 1, 0, n_conv - 1))),
        ],
        out_specs=[
            pl.BlockSpec((n_slab, tm, LANES), lambda i, j: (jnp.clip(j - 1, 0, n_conv - 1), i, 0)),
            pl.BlockSpec((tm, tn), lambda i, j: (i, jnp.maximum(j - n_conv, 0))),
            pl.BlockSpec((GATE_W, tm), lambda i, j: (0, i)),
        ],
        out_shape=[
            jax.ShapeDtypeStruct((GDN_QKV_W // LANES, m, LANES), F32),
            jax.ShapeDtypeStruct((m, D_PROJ - GDN_QKV_W), F32),
            jax.ShapeDtypeStruct((GATE_W, m), F32),
        ],
        scratch_shapes=[
            pltpu.VMEM((tm, d), BF16),
            pltpu.VMEM((n_slab, tm + SUBLANES, LANES), F32),
            pltpu.VMEM((n_slab, tm + SUBLANES, LANES), F32),
            pltpu.VMEM((n_conv, n_slab, SUBLANES, LANES), F32),
        ],
        compiler_params=pltpu.CompilerParams(
            dimension_semantics=("arbitrary", "arbitrary"), vmem_limit_bytes=VMEM_LIMIT),
        name="in_proj",
    )(x, nw, w, w, wgt, conv_w)


def _chunk_iotas():
    row = lax.broadcasted_iota(jnp.int32, (CHUNK, CHUNK), 0)
    col = lax.broadcasted_iota(jnp.int32, (CHUNK, CHUNK), 1)
    return row, col


def _gate_orientations(rows, row, col):
    tril = (row >= col).astype(F32)
    eye = (row == col).astype(F32)
    triu = (row <= col).astype(F32)
    both = _dot_nt(jnp.concatenate([tril, eye], axis=0), rows, precision=lax.Precision.HIGHEST)
    return _dot_hi(rows, triu), both[:CHUNK], both[CHUNK:]


def _lanes(x, c):
    return x[:, c * CHUNK:(c + 1) * CHUNK]


def _unit_lower_inverse(m, row, col, between):
    strict = row > col
    eye = (row == col).astype(F32)
    first = strict & (jnp.right_shift(row, 1) == jnp.right_shift(col, 1))
    t_inv = [eye - jnp.where(first, mi, 0.0) for mi in m]
    s = 1
    while (2 << s) <= CHUNK:
        join = (strict & (jnp.right_shift(row, s + 1) == jnp.right_shift(col, s + 1))
                & (jnp.right_shift(row, s) != jnp.right_shift(col, s)))
        cb = [jnp.where(join, mi, 0.0).astype(BF16) for mi in m]
        tb = [ti.astype(BF16) for ti in t_inv]
        x = [_dot(ci, ti).astype(BF16) for ci, ti in zip(cb, tb)]
        for fn in between.get(s, ()):
            fn()
        t_inv = [ti - _dot(tbi, xi) for ti, tbi, xi in zip(t_inv, tb, x)]
        for fn in between.get(s + 0.5, ()):
            fn()
        s += 1
    return t_inv


def _mixer_kernel(qkv_ref, z_ref, mq_ref, mk_ref, mv_ref, mo_ref, gr_ref, pcol_ref, nw_ref,
                  y_ref, s_ref, c_ref, n_ref, m_ref):
    t = pl.program_id(1)

    @pl.when(t == 0)
    def _():
        s_ref[...] = jnp.zeros(s_ref.shape, F32)
        c_ref[...] = jnp.zeros(c_ref.shape, F32)
        n_ref[...] = jnp.zeros(n_ref.shape, F32)
        m_ref[...] = jnp.zeros(m_ref.shape, F32)

    row, col = _chunk_iotas()
    causal = row >= col
    strict = row > col

    def chunk_rows(ref, c, c0, width):
        return ref[c * CHUNK:(c + 1) * CHUNK, c0:c0 + width]

    g_row = (-jnp.exp(pcol_ref[GA0:GA0 + GDN_HEADS, 1:2])
             * _softplus(gr_ref[GA0:GA0 + GDN_HEADS, :] + pcol_ref[GA0:GA0 + GDN_HEADS, 0:1]))
    beta_row = jax.nn.sigmoid(gr_ref[GB0:GB0 + GDN_HEADS, :])
    g_gates = jnp.concatenate([g_row, beta_row], axis=0)
    pre = gr_ref[MI0:MI0 + 2 * ML_HEADS, :] + pcol_ref[MI0:MI0 + 2 * ML_HEADS, 0:1]
    capped = GATE_SOFTCAP * jnp.tanh(pre / GATE_SOFTCAP)
    is_i = lax.broadcasted_iota(jnp.int32, capped.shape, 0) < ML_HEADS
    m_gates = jnp.where(is_i, capped, -_softplus(-capped))
    g_go = [_gate_orientations(_lanes(g_gates, c), row, col) for c in range(NCB)]
    m_go = [_gate_orientations(_lanes(m_gates, c), row, col) for c in range(NCB)]


    heads = range(GDN_HEADS)
    probs = [(c, h) for c in range(NCB) for h in heads]
    q = [qkv_ref[h, c * CHUNK:(c + 1) * CHUNK, :] for c, h in probs]
    k = [qkv_ref[GDN_HEADS + h, c * CHUNK:(c + 1) * CHUNK, :] for c, h in probs]
    vb = [qkv_ref[2 * GDN_HEADS + h, c * CHUNK:(c + 1) * CHUNK, :].astype(BF16) for c, h in probs]
    kb = [ki.astype(BF16) for ki in k]
    kt = [ki.T for ki in k]
    gc_r = [g_go[c][0][h:h + 1, :] for c, h in probs]
    gc_c = [g_go[c][1][:, h:h + 1] for c, h in probs]
    beta_c = [g_go[c][2][:, GB0 + h:GB0 + h + 1] for c, h in probs]
    beta_r = [_lanes(beta_row, c)[h:h + 1, :] for c, h in probs]
    decay = [jnp.exp(jnp.where(causal, a - b, NEG)) for a, b in zip(gc_c, gc_r)]
    kq = [_dot(jnp.concatenate([kbi, qi.astype(BF16)], axis=0), kti.astype(BF16))
          for kbi, qi, kti in zip(kb, q, kt)]
    m = [jnp.where(strict, x[:CHUNK] * d * b, 0.0) for x, d, b in zip(kq, decay, beta_c)]
    attn = [(x[CHUNK:] * d).astype(BF16) for x, d in zip(kq, decay)]

    mheads = range(ML_HEADS)
    mprobs = [(c, h) for c in range(NCB) for h in mheads]
    mq = [chunk_rows(mq_ref, c, h * ML_DQK, ML_DQK) * (ML_DQK ** -0.5) for c, h in mprobs]
    mk = [chunk_rows(mk_ref, c, h * ML_DQK, ML_DQK) for c, h in mprobs]
    mkt = [x.T for x in mk]
    mqb = [x.astype(BF16) for x in mq]
    mvb = [chunk_rows(mv_ref, c, h * ML_DV, ML_DV).astype(BF16) for c, h in mprobs]
    b_r = [m_go[c][0][ML_HEADS + h:ML_HEADS + h + 1, :] for c, h in mprobs]
    b_c = [m_go[c][1][:, ML_HEADS + h:ML_HEADS + h + 1] for c, h in mprobs]
    i_c = [m_go[c][2][:, h:h + 1] for c, h in mprobs]
    i_r = [_lanes(m_gates, c)[h:h + 1, :] for c, h in mprobs]
    b_end = [x[:, CHUNK - 1:CHUNK] for x in b_r]
    dmat = [jnp.where(causal, bc - br + ir, NEG) for bc, br, ir in zip(b_c, b_r, i_r)]
    m_intra = [jnp.max(d, axis=-1, keepdims=True) for d in dmat]
    mqk = [_dot(a, b.astype(BF16)) for a, b in zip(mqb, mkt)]

    ml_state = {
        "c": [c_ref[h] for h in mheads],
        "n": [n_ref[h:h + 1, :] for h in mheads],
        "m": [m_ref[h:h + 1, 0:1] for h in mheads],
    }

    def mlstm_chunk(c):
        c_st, n_st, m_st = ml_state["c"], ml_state["n"], ml_state["m"]
        ids = [c * ML_HEADS + h for h in mheads]
        inter_log = [b_c[i] + m_st[h] for h, i in zip(mheads, ids)]
        m_t = [jnp.maximum(a, m_intra[i]) for a, i in zip(inter_log, ids)]
        inter = [jnp.exp(a - b) for a, b in zip(inter_log, m_t)]
        p = [jnp.exp(dmat[i] - mt) * mqk[i] for mt, i in zip(m_t, ids)]
        num = [it * _dot(mqb[i], c_st[h].astype(BF16)) + _dot(pi.astype(BF16), mvb[i])
               for h, i, it, pi in zip(mheads, ids, inter, p)]
        den = [it * jnp.sum(mq[i] * n_st[h], axis=-1, keepdims=True) + jnp.sum(pi, axis=-1, keepdims=True)
               for h, i, it, pi in zip(mheads, ids, inter, p)]
        hout = [nu / jnp.maximum(jnp.abs(de), jnp.exp(-mt)) for nu, de, mt in zip(num, den, m_t)]
        m_new = [mt[CHUNK - 1:CHUNK, :] for mt in m_t]
        carry = [jnp.exp(b_end[i] + m_st[h] - mn) for h, i, mn in zip(mheads, ids, m_new)]
        ktw = [(mkt[i] * jnp.exp(b_end[i] - b_r[i] + i_r[i] - mn)).astype(BF16) for i, mn in zip(ids, m_new)]
        kw = [mk[i] * jnp.exp(b_end[i] - b_c[i] + i_c[i] - mn) for i, mn in zip(ids, m_new)]
        ml_state["c"] = [ca * c_st[h] + _dot(x, mvb[i]) for h, i, ca, x in zip(mheads, ids, carry, ktw)]
        ml_state["n"] = [ca * n_st[h] + jnp.sum(x, axis=0, keepdims=True) for h, ca, x in zip(mheads, carry, kw)]
        ml_state["m"] = m_new
        for h, ho in zip(mheads, hout):
            og = chunk_rows(mo_ref, c, h * ML_DV, ML_DV)
            y_ref[c * CHUNK:(c + 1) * CHUNK, GDN_V_W + h * ML_DV:GDN_V_W + (h + 1) * ML_DV] = (
                jax.nn.sigmoid(og) * ho).astype(y_ref.dtype)

    between = {2 + c: [functools.partial(mlstm_chunk, c)] for c in range(NCB)}
    t_inv = _unit_lower_inverse(m, row, col, between)
    for h in mheads:
        c_ref[h] = ml_state["c"][h]
        n_ref[h:h + 1, :] = ml_state["n"][h]
        m_ref[h:h + 1, :] = jnp.broadcast_to(ml_state["m"][h], (1, LANES))

    tbeta = [ti * b for ti, b in zip(t_inv, beta_r)]
    u = [_dot(x.astype(BF16), vi) for x, vi in zip(tbeta, vb)]
    w = [_dot((x * jnp.exp(g)).astype(BF16), kbi) for x, g, kbi in zip(tbeta, gc_r, kb)]
    gc_last = [g[:, CHUNK - 1:CHUNK] for g in gc_r]
    kt_dec = [(kti * jnp.exp(gl - g)).astype(BF16) for kti, gl, g in zip(kt, gc_last, gc_r)]
    g_end = [jnp.exp(gl) for gl in gc_last]
    wq_lhs = [jnp.concatenate([wi, qi * jnp.exp(g)], axis=0).astype(BF16) for wi, qi, g in zip(w, q, gc_c)]

    st = [s_ref[h] for h in heads]
    o = []
    for c in range(NCB):
        i0 = c * GDN_HEADS
        wq = [_dot(wq_lhs[i0 + h], st[h].astype(BF16)) for h in heads]
        vn = [(u[i0 + h] - wq[h][:CHUNK]).astype(BF16) for h in heads]
        o += [wq[h][CHUNK:] + _dot(attn[i0 + h], vn[h]) for h in heads]
        st = [g_end[i0 + h] * st[h] + _dot(kt_dec[i0 + h], vn[h]) for h in heads]
    for h in heads:
        s_ref[h] = st[h]

    nw = nw_ref[...]
    for (c, h), oi in zip(probs, o):
        oi = oi * lax.rsqrt(jnp.mean(oi * oi, axis=-1, keepdims=True) + NORM_EPS) * nw
        oi = oi * _silu(chunk_rows(z_ref, c, h * GDN_DV, GDN_DV))
        y_ref[c * CHUNK:(c + 1) * CHUNK, h * GDN_DV:(h + 1) * GDN_DV] = oi.astype(y_ref.dtype)


def _mixer(qkv, rest, gt, pcol, norm_w, *, batch, seq):
    nt = seq // TB
    z_w = GDN_V_W

    def rows(b, t):
        return b * nt + t

    return pl.pallas_call(
        _mixer_kernel,
        grid=(batch, nt),
        in_specs=[
            pl.BlockSpec((GDN_QKV_W // LANES, TB, LANES), lambda b, t: (0, rows(b, t), 0)),
            pl.BlockSpec((TB, GDN_V_W), lambda b, t: (rows(b, t), 0)),
            pl.BlockSpec((TB, ML_QK_W), lambda b, t: (rows(b, t), z_w // ML_QK_W)),
            pl.BlockSpec((TB, ML_QK_W), lambda b, t: (rows(b, t), z_w // ML_QK_W + 1)),
            pl.BlockSpec((TB, ML_V_W), lambda b, t: (rows(b, t), (z_w + 2 * ML_QK_W) // ML_V_W)),
            pl.BlockSpec((TB, ML_V_W), lambda b, t: (rows(b, t), (z_w + 2 * ML_QK_W) // ML_V_W + 1)),
            pl.BlockSpec((GATE_W, TB), lambda b, t: (0, rows(b, t))),
            pl.BlockSpec((GATE_W, LANES), lambda b, t: (0, 0)),
            pl.BlockSpec((1, GDN_DV), lambda b, t: (0, 0)),
        ],
        out_specs=pl.BlockSpec((TB, D_MIX), lambda b, t: (rows(b, t), 0)),
        out_shape=jax.ShapeDtypeStruct((batch * seq, D_MIX), BF16),
        scratch_shapes=[
            pltpu.VMEM((GDN_HEADS, GDN_DK, GDN_DV), F32),
            pltpu.VMEM((ML_HEADS, ML_DQK, ML_DV), F32),
            pltpu.VMEM((SUBLANES, ML_DQK), F32),
            pltpu.VMEM((SUBLANES, LANES), F32),
        ],
        compiler_params=pltpu.CompilerParams(
            dimension_semantics=("parallel", "arbitrary"), vmem_limit_bytes=VMEM_LIMIT),
        name="mixer",
    )(qkv, rest, rest, rest, rest, rest, gt, pcol, norm_w)


def _out_proj_kernel(x_ref, y_ref, w_ref, o_ref, wb_ref):
    @pl.when(pl.program_id(0) == 0)
    def _():
        wb_ref[...] = w_ref[...].astype(BF16)

    o_ref[...] = x_ref[...] + _dot(y_ref[...], wb_ref[...])


def _out_proj(x, y, w, *, layer, tm=512):
    m, d = x.shape
    return pl.pallas_call(
        _out_proj_kernel,
        grid=(m // tm,),
        in_specs=[
            pl.BlockSpec((tm, d), lambda i: (i, 0)),
            pl.BlockSpec((tm, D_MIX), lambda i: (i, 0)),
            pl.BlockSpec((None, D_MIX, d), lambda i: (layer, 0, 0), pipeline_mode=pl.Buffered(1)),
        ],
        out_specs=pl.BlockSpec((tm, d), lambda i: (i, 0)),
        out_shape=jax.ShapeDtypeStruct((m, d), F32),
        scratch_shapes=[pltpu.VMEM((D_MIX, d), BF16)],
        compiler_params=pltpu.CompilerParams(
            dimension_semantics=("arbitrary",), vmem_limit_bytes=VMEM_LIMIT),
        name="out_proj",
    )(x, y, w)


def _pad_rows(v, offset):
    return jnp.zeros((GATE_W,), F32).at[offset:offset + v.shape[0]].set(v.astype(F32))


def kernel(x, ffn1_norm_w, ffn1_w_gate, ffn1_w_up, ffn1_w_down, mix_norm_w, w_in, conv_w, gdn_a_log, gdn_dt_bias, gdn_norm_w, ml_i_bias, ml_f_bias, w_out, ffn2_norm_w, ffn2_w_gate, ffn2_w_up, ffn2_w_down, final_norm_w):
    batch, seq, d = x.shape
    depth = ffn1_norm_w.shape[0]
    h = x.reshape(batch * seq, d)
    fw = final_norm_w.reshape(1, d).astype(F32)
    m_lo = GDN_W + 2 * GDN_HEADS
    for l in range(depth):
        d_in = w_in.shape[2]
        w_proj = jnp.pad(w_in[l], ((0, 0), (0, -d_in % LANES))).astype(BF16)
        gates = jnp.concatenate([w_proj[:, GDN_W:m_lo], w_proj[:, m_lo + ML_W:d_in]], axis=1)
        wgt = jnp.zeros((GATE_W, d), BF16).at[:N_GATE].set(gates.T)
        bias = _pad_rows(gdn_dt_bias[l], GA0) + _pad_rows(ml_i_bias[l], MI0) + _pad_rows(ml_f_bias[l], MF0)
        alog = _pad_rows(gdn_a_log[l], GA0)
        pcol = jnp.zeros((GATE_W, LANES), F32).at[:, 0].set(bias).at[:, 1].set(alog)

        h = _ffn(h, ffn1_norm_w[l].reshape(1, d), ffn1_w_gate, ffn1_w_up, ffn1_w_down, fw,
                 layer=l, final_norm=False)
        qkv, rest, gt = _in_proj(h, mix_norm_w[l].reshape(1, d), w_proj, wgt, conv_w[l].astype(F32), seq=seq)
        y = _mixer(qkv, rest, gt, pcol, gdn_norm_w[l].reshape(1, GDN_DV).astype(F32), batch=batch, seq=seq)
        h = _out_proj(h, y, w_out, layer=l)
        h = _ffn(h, ffn2_norm_w[l].reshape(1, d), ffn2_w_gate, ffn2_w_up, ffn2_w_down, fw,
                 layer=l, final_norm=(l == depth - 1))
    return h.reshape(batch, seq, d)
```

```python
import functools

import jax
import jax.numpy as jnp
from jax import lax
from jax.experimental import pallas as pl
from jax.experimental.pallas import tpu as pltpu

F32 = jnp.float32
BF16 = jnp.bfloat16

D_MODEL = 2048
D_FF = 5632
GDN_HEADS = 8
GDN_DK = 128
GDN_DV = 128
ML_HEADS = 4
ML_DQK = 128
ML_DV = 256
GDN_QK_W = GDN_HEADS * GDN_DK
GDN_V_W = GDN_HEADS * GDN_DV
GDN_QKV_W = 2 * GDN_QK_W + GDN_V_W
ML_QK_W = ML_HEADS * ML_DQK
ML_V_W = ML_HEADS * ML_DV
D_MIX = GDN_V_W + ML_V_W
CONV_WIDTH = 4
CHUNK = 64
NORM_EPS = 1e-6
GATE_SOFTCAP = 15.0

LANES = 128
SUBLANES = 8
MXU_W = 256
GATE_W = LANES
GA0, GB0, MI0, MF0 = 0, GDN_HEADS, 2 * GDN_HEADS, 2 * GDN_HEADS + ML_HEADS
N_GATE = 2 * GDN_HEADS + 2 * ML_HEADS
GDN_W = GDN_QKV_W + GDN_V_W
ML_W = 2 * ML_QK_W + 2 * ML_V_W
C_GZ = GDN_QKV_W
C_MQ = GDN_W
C_MK = C_MQ + ML_QK_W
C_MV = C_MK + ML_QK_W
C_MO = C_MV + ML_V_W
D_PROJ = GDN_W + ML_W

TB = 4 * CHUNK
NCB = TB // CHUNK
CONV_STRIDE = 4
NEG = -1e30
VMEM_LIMIT = 60 * 1024 * 1024


def _rms(x, w):
    return x * lax.rsqrt(jnp.mean(x * x, axis=-1, keepdims=True) + NORM_EPS) * w


def _softplus(x):
    return jnp.maximum(x, 0.0) + jnp.log1p(jnp.exp(-jnp.abs(x)))


def _silu(x):
    return x * jax.nn.sigmoid(x)


def _dot(a, b):
    return jnp.dot(a, b, preferred_element_type=F32)


def _dot_nt(a, b, precision=None):
    return lax.dot_general(a, b, (((1,), (1,)), ((), ())), preferred_element_type=F32, precision=precision)


def _dot_hi(a, b):
    return jnp.dot(a, b, preferred_element_type=F32, precision=lax.Precision.HIGHEST)


def _ffn_kernel(x_ref, nw_ref, wg_hbm, wu_hbm, wd_hbm, fw_ref, o_ref,
                xn_ref, wg_buf, wu_buf, wd_buf, sem, *, layer, nf, tf, final_norm):
    i = pl.program_id(0)
    last_i = pl.num_programs(0) - 1

    def copies(f, slot):
        cols = pl.ds(pl.multiple_of(f * tf, tf), tf)
        return (pltpu.make_async_copy(wg_hbm.at[layer, :, cols], wg_buf.at[slot], sem.at[0, slot]),
                pltpu.make_async_copy(wu_hbm.at[layer, :, cols], wu_buf.at[slot], sem.at[1, slot]),
                pltpu.make_async_copy(wd_hbm.at[layer, cols, :], wd_buf.at[slot], sem.at[2, slot]))

    def start(f, slot):
        for c in copies(f, slot):
            c.start()

    def wait(f, slot):
        for c in copies(f, slot):
            c.wait()

    @pl.when(i == 0)
    def _():
        start(0, 0)

    xn_ref[...] = _rms(x_ref[...], nw_ref[...]).astype(BF16)
    o_ref[...] = jnp.zeros(o_ref.shape, F32)

    def tile(slot):
        wgu = jnp.concatenate([wg_buf[slot].astype(BF16), wu_buf[slot].astype(BF16)], axis=1)
        gu = _dot(xn_ref[...], wgu)
        a = (_silu(gu[:, :tf]) * gu[:, tf:]).astype(BF16)
        o_ref[...] += _dot(a, wd_buf[slot].astype(BF16))

    def pair(g, carry):
        f = 2 * g
        wait(f, 0)
        start(f + 1, 1)
        tile(0)
        wait(f + 1, 1)
        start(lax.rem(f + 2, nf), 0)
        tile(1)
        return carry

    lax.fori_loop(0, nf // 2, pair, 0)

    @pl.when(i == last_i)
    def _():
        wait(0, 0)

    y = x_ref[...] + 0.5 * o_ref[...]
    if final_norm:
        y = _rms(y, fw_ref[...])
    o_ref[...] = y


def _ffn(x, nw, wg, wu, wd, fw, *, layer, final_norm, tm=1024, tf=256):
    m, d = x.shape
    dff = wg.shape[2]
    nf = dff // tf
    assert nf % 2 == 0
    return pl.pallas_call(
        functools.partial(_ffn_kernel, layer=layer, nf=nf, tf=tf, final_norm=final_norm),
        grid=(m // tm,),
        in_specs=[
            pl.BlockSpec((tm, d), lambda i: (i, 0)),
            pl.BlockSpec((1, d), lambda i: (0, 0)),
            pl.BlockSpec(memory_space=pl.ANY),
            pl.BlockSpec(memory_space=pl.ANY),
            pl.BlockSpec(memory_space=pl.ANY),
            pl.BlockSpec((1, d), lambda i: (0, 0)),
        ],
        out_specs=pl.BlockSpec((tm, d), lambda i: (i, 0)),
        out_shape=jax.ShapeDtypeStruct((m, d), F32),
        scratch_shapes=[
            pltpu.VMEM((tm, d), BF16),
            pltpu.VMEM((2, d, tf), F32),
            pltpu.VMEM((2, d, tf), F32),
            pltpu.VMEM((2, tf, d), F32),
            pltpu.SemaphoreType.DMA((3, 2)),
        ],
        compiler_params=pltpu.CompilerParams(
            dimension_semantics=("arbitrary",), vmem_limit_bytes=VMEM_LIMIT),
        name="ffn",
    )(x, nw, wg, wu, wd, fw)


def _in_proj_kernel(x_ref, nw_ref, w_ref, wx_ref, wgt_ref, cw_ref, qkv_ref, rest_ref, gt_ref,
                    xn_ref, scr_a, scr_b, hist_ref, *, shift, tiles_per_seq):
    i = pl.program_id(0)
    j = pl.program_id(1)
    tm = x_ref.shape[0]
    tn = w_ref.shape[1]
    n_slab = tn // LANES

    @pl.when(j == 0)
    def _():
        xn = _rms(x_ref[...], nw_ref[...]).astype(BF16)
        xn_ref[...] = xn
        gt_ref[...] = _dot_nt(wgt_ref[...], xn)

    @pl.when((j == 0) & (i % tiles_per_seq == 0))
    def _():
        hist_ref[...] = jnp.zeros(hist_ref.shape, F32)

    n_mm = tn // MXU_W

    def to_scratch(scr):
        def store(n, p):
            for g in range(MXU_W // LANES):
                scr[n * (MXU_W // LANES) + g, SUBLANES:SUBLANES + tm, :] = p[:, g * LANES:(g + 1) * LANES]
        return store

    def to_rest(n, p):
        rest_ref[:, n * MXU_W:(n + 1) * MXU_W] = p

    def project(store, conv=None):
        for n in range(n_mm):
            store(n, _dot(xn_ref[...], w_ref[:, n * MXU_W:(n + 1) * MXU_W]))
            if conv is not None:
                scr, slot, l2_scale = conv
                conv_from(scr, slot, l2_scale, range(n * n_slab // n_mm, (n + 1) * n_slab // n_mm))

    def conv_from(scr, slot, l2_scale, slabs):
        span = CONV_STRIDE * SUBLANES
        for g in slabs:
            cols = slice(g * LANES, (g + 1) * LANES)
            scr[g, 0:SUBLANES, :] = hist_ref[slot, g]
            hist_ref[slot, g] = scr[g, tm:tm + SUBLANES, :]
            for r0 in range(0, tm, span):
                for v in range(CONV_STRIDE):
                    acc = None
                    for tap in range(CONV_WIDTH):
                        lo = SUBLANES - (CONV_WIDTH - 1) + tap + r0 + v
                        term = scr[g, pl.ds(lo, SUBLANES, stride=CONV_STRIDE), :] * cw_ref[tap:tap + 1, cols]
                        acc = term if acc is None else acc + term
                    y = _silu(acc)
                    if l2_scale is not None:
                        y = y * (lax.rsqrt(jnp.sum(y * y, axis=-1, keepdims=True) + NORM_EPS) * l2_scale)
                    qkv_ref[g, pl.ds(r0 + v, SUBLANES, stride=CONV_STRIDE), :] = y

    @pl.when(j == 0)
    def _():
        project(to_scratch(scr_a))

    @pl.when(j == 1)
    def _():
        project(to_scratch(scr_b), (scr_a, 0, GDN_DK ** -0.5))

    @pl.when(j == 2)
    def _():
        project(to_scratch(scr_a), (scr_b, 1, 1.0))

    @pl.when(j == 3)
    def _():
        project(to_rest, (scr_a, 2, None))

    @pl.when(j > 3)
    def _():
        w = jnp.concatenate([w_ref[...], wx_ref[...]], axis=1)
        w = pltpu.bitcast(pltpu.bitcast(w, jnp.uint32)[:, shift:shift + tn], BF16)
        rest_ref[...] = _dot(xn_ref[...], w)


def _in_proj(x, nw, w, wgt, conv_w, *, seq, tm=1024, tn=GDN_QK_W):
    m, d = x.shape
    assert tn == GDN_QK_W == GDN_V_W and seq % tm == 0
    n_conv = GDN_QKV_W // tn
    n_tiles = D_PROJ // tn
    n_slab = tn // LANES
    return pl.pallas_call(
        functools.partial(_in_proj_kernel, shift=2 * GDN_HEADS, tiles_per_seq=seq // tm),
        grid=(m // tm, n_tiles),
        in_specs=[
            pl.BlockSpec((tm, d), lambda i, j: (i, 0)),
            pl.BlockSpec((1, d), lambda i, j: (0, 0)),
            pl.BlockSpec((d, tn), lambda i, j: (0, j)),
            pl.BlockSpec((d, LANES), lambda i, j: (0, (j + 1) * n_slab)),
            pl.BlockSpec((GATE_W, d), lambda i, j: (0, 0)),
            pl.BlockSpec((CONV_WIDTH, tn), lambda i, j: (0, jnp.clip(j - 1, 0, n_conv - 1))),
        ],
        out_specs=[
            pl.BlockSpec((n_slab, tm, LANES), lambda i, j: (jnp.clip(j - 1, 0, n_conv - 1), i, 0)),
            pl.BlockSpec((tm, tn), lambda i, j: (i, jnp.maximum(j - n_conv, 0))),
            pl.BlockSpec((GATE_W, tm), lambda i, j: (0, i)),
        ],
        out_shape=[
            jax.ShapeDtypeStruct((GDN_QKV_W // LANES, m, LANES), F32),
            jax.ShapeDtypeStruct((m, D_PROJ - GDN_QKV_W), F32),
            jax.ShapeDtypeStruct((GATE_W, m), F32),
        ],
        scratch_shapes=[
            pltpu.VMEM((tm, d), BF16),
            pltpu.VMEM((n_slab, tm + SUBLANES, LANES), F32),
            pltpu.VMEM((n_slab, tm + SUBLANES, LANES), F32),
            pltpu.VMEM((n_conv, n_slab, SUBLANES, LANES), F32),
        ],
        compiler_params=pltpu.CompilerParams(
            dimension_semantics=("arbitrary", "arbitrary"), vmem_limit_bytes=VMEM_LIMIT),
        name="in_proj",
    )(x, nw, w, w, wgt, conv_w)


def _chunk_iotas():
    row = lax.broadcasted_iota(jnp.int32, (CHUNK, CHUNK), 0)
    col = lax.broadcasted_iota(jnp.int32, (CHUNK, CHUNK), 1)
    return row, col


def _gate_orientations(rows, row, col):
    tril = (row >= col).astype(F32)
    eye = (row == col).astype(F32)
    triu = (row <= col).astype(F32)
    both = _dot_nt(jnp.concatenate([tril, eye], axis=0), rows, precision=lax.Precision.HIGHEST)
    return _dot_hi(rows, triu), both[:CHUNK], both[CHUNK:]


def _lanes(x, c):
    return x[:, c * CHUNK:(c + 1) * CHUNK]


def _unit_lower_inverse(m, row, col, between):
    strict = row > col
    eye = (row == col).astype(F32)
    first = strict & (jnp.right_shift(row, 1) == jnp.right_shift(col, 1))
    t_inv = [eye - jnp.where(first, mi, 0.0) for mi in m]
    s = 1
    while (2 << s) <= CHUNK:
        join = (strict & (jnp.right_shift(row, s + 1) == jnp.right_shift(col, s + 1))
                & (jnp.right_shift(row, s) != jnp.right_shift(col, s)))
        cb = [jnp.where(join, mi, 0.0).astype(BF16) for mi in m]
        tb = [ti.astype(BF16) for ti in t_inv]
        x = [_dot(ci, ti).astype(BF16) for ci, ti in zip(cb, tb)]
        for fn in between.get(s, ()):
            fn()
        t_inv = [ti - _dot(tbi, xi) for ti, tbi, xi in zip(t_inv, tb, x)]
        for fn in between.get(s + 0.5, ()):
            fn()
        s += 1
    return t_inv


def _mixer_kernel(qkv_ref, z_ref, mq_ref, mk_ref, mv_ref, mo_ref, gr_ref, pcol_ref, nw_ref,
                  y_ref, s_ref, c_ref, n_ref, m_ref):
    t = pl.program_id(1)

    @pl.when(t == 0)
    def _():
        s_ref[...] = jnp.zeros(s_ref.shape, F32)
        c_ref[...] = jnp.zeros(c_ref.shape, F32)
        n_ref[...] = jnp.zeros(n_ref.shape, F32)
        m_ref[...] = jnp.zeros(m_ref.shape, F32)

    row, col = _chunk_iotas()
    causal = row >= col
    strict = row > col

    def chunk_rows(ref, c, c0, width):
        return ref[c * CHUNK:(c + 1) * CHUNK, c0:c0 + width]

    g_row = (-jnp.exp(pcol_ref[GA0:GA0 + GDN_HEADS, 1:2])
             * _softplus(gr_ref[GA0:GA0 + GDN_HEADS, :] + pcol_ref[GA0:GA0 + GDN_HEADS, 0:1]))
    beta_row = jax.nn.sigmoid(gr_ref[GB0:GB0 + GDN_HEADS, :])
    g_gates = jnp.concatenate([g_row, beta_row], axis=0)
    pre = gr_ref[MI0:MI0 + 2 * ML_HEADS, :] + pcol_ref[MI0:MI0 + 2 * ML_HEADS, 0:1]
    capped = GATE_SOFTCAP * jnp.tanh(pre / GATE_SOFTCAP)
    is_i = lax.broadcasted_iota(jnp.int32, capped.shape, 0) < ML_HEADS
    m_gates = jnp.where(is_i, capped, -_softplus(-capped))
    g_go = [_gate_orientations(_lanes(g_gates, c), row, col) for c in range(NCB)]
    m_go = [_gate_orientations(_lanes(m_gates, c), row, col) for c in range(NCB)]


    heads = range(GDN_HEADS)
    probs = [(c, h) for c in range(NCB) for h in heads]
    q = [qkv_ref[h, c * CHUNK:(c + 1) * CHUNK, :] for c, h in probs]
    k = [qkv_ref[GDN_HEADS + h, c * CHUNK:(c + 1) * CHUNK, :] for c, h in probs]
    vb = [qkv_ref[2 * GDN_HEADS + h, c * CHUNK:(c + 1) * CHUNK, :].astype(BF16) for c, h in probs]
    kb = [ki.astype(BF16) for ki in k]
    kt = [ki.T for ki in k]
    gc_r = [g_go[c][0][h:h + 1, :] for c, h in probs]
    gc_c = [g_go[c][1][:, h:h + 1] for c, h in probs]
    beta_c = [g_go[c][2][:, GB0 + h:GB0 + h + 1] for c, h in probs]
    beta_r = [_lanes(beta_row, c)[h:h + 1, :] for c, h in probs]
    decay = [jnp.exp(jnp.where(causal, a - b, NEG)) for a, b in zip(gc_c, gc_r)]
    kq = [_dot(jnp.concatenate([kbi, qi.astype(BF16)], axis=0), kti.astype(BF16))
          for kbi, qi, kti in zip(kb, q, kt)]
    m = [jnp.where(strict, x[:CHUNK] * d * b, 0.0) for x, d, b in zip(kq, decay, beta_c)]
    attn = [(x[CHUNK:] * d).astype(BF16) for x, d in zip(kq, decay)]

    mheads = range(ML_HEADS)
    mprobs = [(c, h) for c in range(NCB) for h in mheads]
    mq = [chunk_rows(mq_ref, c, h * ML_DQK, ML_DQK) * (ML_DQK ** -0.5) for c, h in mprobs]
    mk = [chunk_rows(mk_ref, c, h * ML_DQK, ML_DQK) for c, h in mprobs]
    mkt = [x.T for x in mk]
    mqb = [x.astype(BF16) for x in mq]
    mvb = [chunk_rows(mv_ref, c, h * ML_DV, ML_DV).astype(BF16) for c, h in mprobs]
    b_r = [m_go[c][0][ML_HEADS + h:ML_HEADS + h + 1, :] for c, h in mprobs]
    b_c = [m_go[c][1][:, ML_HEADS + h:ML_HEADS + h + 1] for c, h in mprobs]
    i_c = [m_go[c][2][:, h:h + 1] for c, h in mprobs]
    i_r = [_lanes(m_gates, c)[h:h + 1, :] for c, h in mprobs]
    b_end = [x[:, CHUNK - 1:CHUNK] for x in b_r]
    dmat = [jnp.where(causal, bc - br + ir, NEG) for bc, br, ir in zip(b_c, b_r, i_r)]
    m_intra = [jnp.max(d, axis=-1, keepdims=True) for d in dmat]
    mqk = [_dot(a, b.astype(BF16)) for a, b in zip(mqb, mkt)]

    ml_state = {
        "c": [c_ref[h] for h in mheads],
        "n": [n_ref[h:h + 1, :] for h in mheads],
        "m": [m_ref[h:h + 1, 0:1] for h in mheads],
    }

    def mlstm_chunk(c):
        c_st, n_st, m_st = ml_state["c"], ml_state["n"], ml_state["m"]
        ids = [c * ML_HEADS + h for h in mheads]
        inter_log = [b_c[i] + m_st[h] for h, i in zip(mheads, ids)]
        m_t = [jnp.maximum(a, m_intra[i]) for a, i in zip(inter_log, ids)]
        inter = [jnp.exp(a - b) for a, b in zip(inter_log, m_t)]
        p = [jnp.exp(dmat[i] - mt) * mqk[i] for mt, i in zip(m_t, ids)]
        num = [it * _dot(mqb[i], c_st[h].astype(BF16)) + _dot(pi.astype(BF16), mvb[i])
               for h, i, it, pi in zip(mheads, ids, inter, p)]
        den = [it * jnp.sum(mq[i] * n_st[h], axis=-1, keepdims=True) + jnp.sum(pi, axis=-1, keepdims=True)
               for h, i, it, pi in zip(mheads, ids, inter, p)]
        hout = [nu / jnp.maximum(jnp.abs(de), jnp.exp(-mt)) for nu, de, mt in zip(num, den, m_t)]
        m_new = [mt[CHUNK - 1:CHUNK, :] for mt in m_t]
        carry = [jnp.exp(b_end[i] + m_st[h] - mn) for h, i, mn in zip(mheads, ids, m_new)]
        ktw = [(mkt[i] * jnp.exp(b_end[i] - b_r[i] + i_r[i] - mn)).astype(BF16) for i, mn in zip(ids, m_new)]
        kw = [mk[i] * jnp.exp(b_end[i] - b_c[i] + i_c[i] - mn) for i, mn in zip(ids, m_new)]
        ml_state["c"] = [ca * c_st[h] + _dot(x, mvb[i]) for h, i, ca, x in zip(mheads, ids, carry, ktw)]
        ml_state["n"] = [ca * n_st[h] + jnp.sum(x, axis=0, keepdims=True) for h, ca, x in zip(mheads, carry, kw)]
        ml_state["m"] = m_new
        for h, ho in zip(mheads, hout):
            og = chunk_rows(mo_ref, c, h * ML_DV, ML_DV)
            y_ref[c * CHUNK:(c + 1) * CHUNK, GDN_V_W + h * ML_DV:GDN_V_W + (h + 1) * ML_DV] = (
                jax.nn.sigmoid(og) * ho).astype(y_ref.dtype)

    between = {2 + c: [functools.partial(mlstm_chunk, c)] for c in range(NCB)}
    t_inv = _unit_lower_inverse(m, row, col, between)
    for h in mheads:
        c_ref[h] = ml_state["c"][h]
        n_ref[h:h + 1, :] = ml_state["n"][h]
        m_ref[h:h + 1, :] = jnp.broadcast_to(ml_state["m"][h], (1, LANES))

    tbeta = [ti * b for ti, b in zip(t_inv, beta_r)]
    u = [_dot(x.astype(BF16), vi) for x, vi in zip(tbeta, vb)]
    w = [_dot((x * jnp.exp(g)).astype(BF16), kbi) for x, g, kbi in zip(tbeta, gc_r, kb)]
    gc_last = [g[:, CHUNK - 1:CHUNK] for g in gc_r]
    kt_dec = [(kti * jnp.exp(gl - g)).astype(BF16) for kti, gl, g in zip(kt, gc_last, gc_r)]
    g_end = [jnp.exp(gl) for gl in gc_last]
    wq_lhs = [jnp.concatenate([wi, qi * jnp.exp(g)], axis=0).astype(BF16) for wi, qi, g in zip(w, q, gc_c)]

    st = [s_ref[h] for h in heads]
    o = []
    for c in range(NCB):
        i0 = c * GDN_HEADS
        wq = [_dot(wq_lhs[i0 + h], st[h].astype(BF16)) for h in heads]
        vn = [(u[i0 + h] - wq[h][:CHUNK]).astype(BF16) for h in heads]
        o += [wq[h][CHUNK:] + _dot(attn[i0 + h], vn[h]) for h in heads]
        st = [g_end[i0 + h] * st[h] + _dot(kt_dec[i0 + h], vn[h]) for h in heads]
    for h in heads:
        s_ref[h] = st[h]

    nw = nw_ref[...]
    for (c, h), oi in zip(probs, o):
        oi = oi * lax.rsqrt(jnp.mean(oi * oi, axis=-1, keepdims=True) + NORM_EPS) * nw
        oi = oi * _silu(chunk_rows(z_ref, c, h * GDN_DV, GDN_DV))
        y_ref[c * CHUNK:(c + 1) * CHUNK, h * GDN_DV:(h + 1) * GDN_DV] = oi.astype(y_ref.dtype)


def _mixer(qkv, rest, gt, pcol, norm_w, *, batch, seq):
    nt = seq // TB
    z_w = GDN_V_W

    def rows(b, t):
        return b * nt + t

    return pl.pallas_call(
        _mixer_kernel,
        grid=(batch, nt),
        in_specs=[
            pl.BlockSpec((GDN_QKV_W // LANES, TB, LANES), lambda b, t: (0, rows(b, t), 0)),
            pl.BlockSpec((TB, GDN_V_W), lambda b, t: (rows(b, t), 0)),
            pl.BlockSpec((TB, ML_QK_W), lambda b, t: (rows(b, t), z_w // ML_QK_W)),
            pl.BlockSpec((TB, ML_QK_W), lambda b, t: (rows(b, t), z_w // ML_QK_W + 1)),
            pl.BlockSpec((TB, ML_V_W), lambda b, t: (rows(b, t), (z_w + 2 * ML_QK_W) // ML_V_W)),
            pl.BlockSpec((TB, ML_V_W), lambda b, t: (rows(b, t), (z_w + 2 * ML_QK_W) // ML_V_W + 1)),
            pl.BlockSpec((GATE_W, TB), lambda b, t: (0, rows(b, t))),
            pl.BlockSpec((GATE_W, LANES), lambda b, t: (0, 0)),
            pl.BlockSpec((1, GDN_DV), lambda b, t: (0, 0)),
        ],
        out_specs=pl.BlockSpec((TB, D_MIX), lambda b, t: (rows(b, t), 0)),
        out_shape=jax.ShapeDtypeStruct((batch * seq, D_MIX), BF16),
        scratch_shapes=[
            pltpu.VMEM((GDN_HEADS, GDN_DK, GDN_DV), F32),
            pltpu.VMEM((ML_HEADS, ML_DQK, ML_DV), F32),
            pltpu.VMEM((SUBLANES, ML_DQK), F32),
            pltpu.VMEM((SUBLANES, LANES), F32),
        ],
        compiler_params=pltpu.CompilerParams(
            dimension_semantics=("parallel", "arbitrary"), vmem_limit_bytes=VMEM_LIMIT),
        name="mixer",
    )(qkv, rest, rest, rest, rest, rest, gt, pcol, norm_w)


def _out_proj_kernel(x_ref, y_ref, w_ref, o_ref, wb_ref):
    @pl.when(pl.program_id(0) == 0)
    def _():
        wb_ref[...] = w_ref[...].astype(BF16)

    o_ref[...] = x_ref[...] + _dot(y_ref[...], wb_ref[...])


def _out_proj(x, y, w, *, layer, tm=512):
    m, d = x.shape
    return pl.pallas_call(
        _out_proj_kernel,
        grid=(m // tm,),
        in_specs=[
            pl.BlockSpec((tm, d), lambda i: (i, 0)),
            pl.BlockSpec((tm, D_MIX), lambda i: (i, 0)),
            pl.BlockSpec((None, D_MIX, d), lambda i: (layer, 0, 0), pipeline_mode=pl.Buffered(1)),
        ],
        out_specs=pl.BlockSpec((tm, d), lambda i: (i, 0)),
        out_shape=jax.ShapeDtypeStruct((m, d), F32),
        scratch_shapes=[pltpu.VMEM((D_MIX, d), BF16)],
        compiler_params=pltpu.CompilerParams(
            dimension_semantics=("arbitrary",), vmem_limit_bytes=VMEM_LIMIT),
        name="out_proj",
    )(x, y, w)


def _pad_rows(v, offset):
    return jnp.zeros((GATE_W,), F32).at[offset:offset + v.shape[0]].set(v.astype(F32))


def kernel(x, ffn1_norm_w, ffn1_w_gate, ffn1_w_up, ffn1_w_down, mix_norm_w, w_in, conv_w, gdn_a_log, gdn_dt_bias, gdn_norm_w, ml_i_bias, ml_f_bias, w_out, ffn2_norm_w, ffn2_w_gate, ffn2_w_up, ffn2_w_down, final_norm_w):
    batch, seq, d = x.shape
    depth = ffn1_norm_w.shape[0]
    h = x.reshape(batch * seq, d)
    fw = final_norm_w.reshape(1, d).astype(F32)
    m_lo = GDN_W + 2 * GDN_HEADS
    for l in range(depth):
        d_in = w_in.shape[2]
        w_proj = jnp.pad(w_in[l], ((0, 0), (0, -d_in % LANES))).astype(BF16)
        gates = jnp.concatenate([w_proj[:, GDN_W:m_lo], w_proj[:, m_lo + ML_W:d_in]], axis=1)
        wgt = jnp.zeros((GATE_W, d), BF16).at[:N_GATE].set(gates.T)
        bias = _pad_rows(gdn_dt_bias[l], GA0) + _pad_rows(ml_i_bias[l], MI0) + _pad_rows(ml_f_bias[l], MF0)
        alog = _pad_rows(gdn_a_log[l], GA0)
        pcol = jnp.zeros((GATE_W, LANES), F32).at[:, 0].set(bias).at[:, 1].set(alog)

        h = _ffn(h, ffn1_norm_w[l].reshape(1, d), ffn1_w_gate, ffn1_w_up, ffn1_w_down, fw,
                 layer=l, final_norm=False)
        qkv, rest, gt = _in_proj(h, mix_norm_w[l].reshape(1, d), w_proj, wgt, conv_w[l].astype(F32), seq=seq)
        y = _mixer(qkv, rest, gt, pcol, gdn_norm_w[l].reshape(1, GDN_DV).astype(F32), batch=batch, seq=seq)
        h = _out_proj(h, y, w_out, layer=l)
        h = _ffn(h, ffn2_norm_w[l].reshape(1, d), ffn2_w_gate, ffn2_w_up, ffn2_w_down, fw,
                 layer=l, final_norm=(l == depth - 1))
    return h.reshape(batch, seq, d)
```

```python
import functools

import jax
import jax.numpy as jnp
from jax import lax
from jax.experimental import pallas as pl
from jax.experimental.pallas import tpu as pltpu

F32 = jnp.float32
BF16 = jnp.bfloat16

GDN_HEADS = 8
GDN_DK = 128
GDN_DV = 128
ML_HEADS = 4
ML_DQK = 128
ML_DV = 256
GDN_QK_W = GDN_HEADS * GDN_DK
GDN_V_W = GDN_HEADS * GDN_DV
GDN_QKV_W = 2 * GDN_QK_W + GDN_V_W
ML_QK_W = ML_HEADS * ML_DQK
ML_V_W = ML_HEADS * ML_DV
D_MIX = GDN_V_W + ML_V_W
CONV_WIDTH = 4
CHUNK = 64
NORM_EPS = 1e-6
GATE_SOFTCAP = 15.0

LANES = 128
SUBLANES = 8
MXU_W = 256
GATE_W = LANES
GA0, GB0, MI0, MF0 = 0, GDN_HEADS, 2 * GDN_HEADS, 2 * GDN_HEADS + ML_HEADS
N_GATE = 2 * GDN_HEADS + 2 * ML_HEADS
GDN_W = GDN_QKV_W + GDN_V_W
ML_W = 2 * ML_QK_W + 2 * ML_V_W
D_PROJ = GDN_W + ML_W

TB = 4 * CHUNK
NCB = TB // CHUNK
CONV_STRIDE = 4
NEG = -1e30
VMEM_LIMIT = 60 * 1024 * 1024


def _rms(x, w):
    return x * lax.rsqrt(jnp.mean(x * x, axis=-1, keepdims=True) + NORM_EPS) * w


def _softplus(x):
    return jnp.maximum(x, 0.0) + jnp.log1p(jnp.exp(-jnp.abs(x)))


def _silu(x):
    return x * jax.nn.sigmoid(x)


def _dot(a, b):
    return jnp.dot(a, b, preferred_element_type=F32)


def _dot_nt(a, b, precision=None):
    return lax.dot_general(a, b, (((1,), (1,)), ((), ())), preferred_element_type=F32, precision=precision)


def _dot_hi(a, b):
    return jnp.dot(a, b, preferred_element_type=F32, precision=lax.Precision.HIGHEST)


def _ffn_kernel(x_ref, nw_ref, wg_hbm, wu_hbm, wd_hbm, fw_ref, o_ref,
                xn_ref, wg_buf, wu_buf, wd_buf, sem, *, layer, nf, tf, final_norm):
    i = pl.program_id(0)
    last_i = pl.num_programs(0) - 1

    def copies(f, slot):
        cols = pl.ds(pl.multiple_of(f * tf, tf), tf)
        return (pltpu.make_async_copy(wg_hbm.at[layer, :, cols], wg_buf.at[slot], sem.at[0, slot]),
                pltpu.make_async_copy(wu_hbm.at[layer, :, cols], wu_buf.at[slot], sem.at[1, slot]),
                pltpu.make_async_copy(wd_hbm.at[layer, cols, :], wd_buf.at[slot], sem.at[2, slot]))

    def start(f, slot):
        for c in copies(f, slot):
            c.start()

    def wait(f, slot):
        for c in copies(f, slot):
            c.wait()

    @pl.when(i == 0)
    def _():
        start(0, 0)

    xn_ref[...] = _rms(x_ref[...], nw_ref[...]).astype(BF16)
    o_ref[...] = jnp.zeros(o_ref.shape, F32)

    def tile(slot):
        wgu = jnp.concatenate([wg_buf[slot].astype(BF16), wu_buf[slot].astype(BF16)], axis=1)
        gu = _dot(xn_ref[...], wgu)
        a = (_silu(gu[:, :tf]) * gu[:, tf:]).astype(BF16)
        o_ref[...] += _dot(a, wd_buf[slot].astype(BF16))

    def pair(g, carry):
        f = 2 * g
        wait(f, 0)
        start(f + 1, 1)
        tile(0)
        wait(f + 1, 1)
        start(lax.rem(f + 2, nf), 0)
        tile(1)
        return carry

    lax.fori_loop(0, nf // 2, pair, 0)

    @pl.when(i == last_i)
    def _():
        wait(0, 0)

    y = x_ref[...] + 0.5 * o_ref[...]
    if final_norm:
        y = _rms(y, fw_ref[...])
    o_ref[...] = y


def _ffn(x, nw, wg, wu, wd, fw, *, layer, final_norm, tm=1024, tf=256):
    m, d = x.shape
    dff = wg.shape[2]
    nf = dff // tf
    assert nf % 2 == 0
    return pl.pallas_call(
        functools.partial(_ffn_kernel, layer=layer, nf=nf, tf=tf, final_norm=final_norm),
        grid=(m // tm,),
        in_specs=[
            pl.BlockSpec((tm, d), lambda i: (i, 0)),
            pl.BlockSpec((1, d), lambda i: (0, 0)),
            pl.BlockSpec(memory_space=pl.ANY),
            pl.BlockSpec(memory_space=pl.ANY),
            pl.BlockSpec(memory_space=pl.ANY),
            pl.BlockSpec((1, d), lambda i: (0, 0)),
        ],
        out_specs=pl.BlockSpec((tm, d), lambda i: (i, 0)),
        out_shape=jax.ShapeDtypeStruct((m, d), F32),
        scratch_shapes=[
            pltpu.VMEM((tm, d), BF16),
            pltpu.VMEM((2, d, tf), F32),
            pltpu.VMEM((2, d, tf), F32),
            pltpu.VMEM((2, tf, d), F32),
            pltpu.SemaphoreType.DMA((3, 2)),
        ],
        compiler_params=pltpu.CompilerParams(
            dimension_semantics=("arbitrary",), vmem_limit_bytes=VMEM_LIMIT),
        name="ffn",
    )(x, nw, wg, wu, wd, fw)


def _in_proj_kernel(x_ref, nw_ref, w_ref, wx_ref, wgt_ref, cw_ref, qkv_ref, rest_ref, gt_ref,
                    xn_ref, scr_a, scr_b, hist_ref, *, shift, tiles_per_seq):
    i = pl.program_id(0)
    j = pl.program_id(1)
    tm = x_ref.shape[0]
    tn = w_ref.shape[1]
    n_slab = tn // LANES

    @pl.when(j == 0)
    def _():
        xn = _rms(x_ref[...], nw_ref[...]).astype(BF16)
        xn_ref[...] = xn
        gt_ref[...] = _dot_nt(wgt_ref[...], xn)

    @pl.when((j == 0) & (i % tiles_per_seq == 0))
    def _():
        hist_ref[...] = jnp.zeros(hist_ref.shape, F32)

    n_mm = tn // MXU_W

    def to_scratch(scr):
        def store(n, p):
            for g in range(MXU_W // LANES):
                scr[n * (MXU_W // LANES) + g, SUBLANES:SUBLANES + tm, :] = p[:, g * LANES:(g + 1) * LANES]
        return store

    def to_rest(n, p):
        rest_ref[:, n * MXU_W:(n + 1) * MXU_W] = p

    def project(store, conv=None):
        for n in range(n_mm):
            store(n, _dot(xn_ref[...], w_ref[:, n * MXU_W:(n + 1) * MXU_W]))
            if conv is not None:
                scr, slot, l2_scale = conv
                conv_from(scr, slot, l2_scale, range(n * n_slab // n_mm, (n + 1) * n_slab // n_mm))

    def conv_from(scr, slot, l2_scale, slabs):
        span = CONV_STRIDE * SUBLANES
        for g in slabs:
            cols = slice(g * LANES, (g + 1) * LANES)
            scr[g, 0:SUBLANES, :] = hist_ref[slot, g]
            hist_ref[slot, g] = scr[g, tm:tm + SUBLANES, :]
            for r0 in range(0, tm, span):
                for v in range(CONV_STRIDE):
                    acc = None
                    for tap in range(CONV_WIDTH):
                        lo = SUBLANES - (CONV_WIDTH - 1) + tap + r0 + v
                        term = scr[g, pl.ds(lo, SUBLANES, stride=CONV_STRIDE), :] * cw_ref[tap:tap + 1, cols]
                        acc = term if acc is None else acc + term
                    y = _silu(acc)
                    if l2_scale is not None:
                        y = y * (lax.rsqrt(jnp.sum(y * y, axis=-1, keepdims=True) + NORM_EPS) * l2_scale)
                    qkv_ref[g, pl.ds(r0 + v, SUBLANES, stride=CONV_STRIDE), :] = y

    @pl.when(j == 0)
    def _():
        project(to_scratch(scr_a))

    @pl.when(j == 1)
    def _():
        project(to_scratch(scr_b), (scr_a, 0, GDN_DK ** -0.5))

    @pl.when(j == 2)
    def _():
        project(to_scratch(scr_a), (scr_b, 1, 1.0))

    @pl.when(j == 3)
    def _():
        project(to_rest, (scr_a, 2, None))

    @pl.when(j > 3)
    def _():
        w = jnp.concatenate([w_ref[...], wx_ref[...]], axis=1)
        w = pltpu.bitcast(pltpu.bitcast(w, jnp.uint32)[:, shift:shift + tn], BF16)
        rest_ref[...] = _dot(xn_ref[...], w)


def _in_proj(x, nw, w, wgt, conv_w, *, seq, tm=1024, tn=GDN_QK_W):
    m, d = x.shape
    assert tn == GDN_QK_W == GDN_V_W and seq % tm == 0
    n_conv = GDN_QKV_W // tn
    n_tiles = D_PROJ // tn
    n_slab = tn // LANES
    return pl.pallas_call(
        functools.partial(_in_proj_kernel, shift=2 * GDN_HEADS, tiles_per_seq=seq // tm),
        grid=(m // tm, n_tiles),
        in_specs=[
            pl.BlockSpec((tm, d), lambda i, j: (i, 0)),
            pl.BlockSpec((1, d), lambda i, j: (0, 0)),
            pl.BlockSpec((d, tn), lambda i, j: (0, j)),
            pl.BlockSpec((d, LANES), lambda i, j: (0, (j + 1) * n_slab)),
            pl.BlockSpec((GATE_W, d), lambda i, j: (0, 0)),
            pl.BlockSpec((CONV_WIDTH, tn), lambda i, j: (0, jnp.clip(j - 1, 0, n_conv - 1))),
        ],
        out_specs=[
            pl.BlockSpec((n_slab, tm, LANES), lambda i, j: (jnp.clip(j - 1, 0, n_conv - 1), i, 0)),
            pl.BlockSpec((tm, tn), lambda i, j: (i, jnp.maximum(j - n_conv, 0))),
            pl.BlockSpec((GATE_W, tm), lambda i, j: (0, i)),
        ],
        out_shape=[
            jax.ShapeDtypeStruct((GDN_QKV_W // LANES, m, LANES), F32),
            jax.ShapeDtypeStruct((m, D_PROJ - GDN_QKV_W), F32),
            jax.ShapeDtypeStruct((GATE_W, m), F32),
        ],
        scratch_shapes=[
            pltpu.VMEM((tm, d), BF16),
            pltpu.VMEM((n_slab, tm + SUBLANES, LANES), F32),
            pltpu.VMEM((n_slab, tm + SUBLANES, LANES), F32),
            pltpu.VMEM((n_conv, n_slab, SUBLANES, LANES), F32),
        ],
        compiler_params=pltpu.CompilerParams(
            dimension_semantics=("arbitrary", "arbitrary"), vmem_limit_bytes=VMEM_LIMIT),
        name="in_proj",
    )(x, nw, w, w, wgt, conv_w)


def _chunk_iotas():
    row = lax.broadcasted_iota(jnp.int32, (CHUNK, CHUNK), 0)
    col = lax.broadcasted_iota(jnp.int32, (CHUNK, CHUNK), 1)
    return row, col


def _gate_orientations(rows, row, col):
    tril = (row >= col).astype(F32)
    eye = (row == col).astype(F32)
    triu = (row <= col).astype(F32)
    both = _dot_nt(jnp.concatenate([tril, eye], axis=0), rows, precision=lax.Precision.HIGHEST)
    return _dot_hi(rows, triu), both[:CHUNK], both[CHUNK:]


def _lanes(x, c):
    return x[:, c * CHUNK:(c + 1) * CHUNK]


def _unit_lower_inverse(m, row, col, between):
    strict = row > col
    eye = (row == col).astype(F32)
    first = strict & (jnp.right_shift(row, 1) == jnp.right_shift(col, 1))
    t_inv = [eye - jnp.where(first, mi, 0.0) for mi in m]
    m = [mi.astype(BF16) for mi in m]
    s = 1
    while (2 << s) <= CHUNK:
        join = (strict & (jnp.right_shift(row, s + 1) == jnp.right_shift(col, s + 1))
                & (jnp.right_shift(row, s) != jnp.right_shift(col, s)))
        cb = [jnp.where(join, mi, jnp.zeros_like(mi)) for mi in m]
        tb = [ti.astype(BF16) for ti in t_inv]
        x = [_dot(ci, ti).astype(BF16) for ci, ti in zip(cb, tb)]
        for fn in between.get(s, ()):
            fn()
        t_inv = [ti - _dot(tbi, xi) for ti, tbi, xi in zip(t_inv, tb, x)]
        for fn in between.get(s + 0.5, ()):
            fn()
        s += 1
    return t_inv


def _mixer_kernel(qkv_ref, z_ref, mq_ref, mk_ref, mv_ref, mo_ref, gr_ref, pcol_ref, nw_ref,
                  y_ref, s_ref, c_ref, n_ref, m_ref):
    t = pl.program_id(1)

    @pl.when(t == 0)
    def _():
        s_ref[...] = jnp.zeros(s_ref.shape, F32)
        c_ref[...] = jnp.zeros(c_ref.shape, F32)
        n_ref[...] = jnp.zeros(n_ref.shape, F32)
        m_ref[...] = jnp.zeros(m_ref.shape, F32)

    row, col = _chunk_iotas()
    causal = row >= col
    strict = row > col

    def chunk_rows(ref, c, c0, width):
        return ref[c * CHUNK:(c + 1) * CHUNK, c0:c0 + width]

    g_row = (-jnp.exp(pcol_ref[GA0:GA0 + GDN_HEADS, 1:2])
             * _softplus(gr_ref[GA0:GA0 + GDN_HEADS, :] + pcol_ref[GA0:GA0 + GDN_HEADS, 0:1]))
    beta_row = jax.nn.sigmoid(gr_ref[GB0:GB0 + GDN_HEADS, :])
    g_gates = jnp.concatenate([g_row, beta_row], axis=0)
    pre = gr_ref[MI0:MI0 + 2 * ML_HEADS, :] + pcol_ref[MI0:MI0 + 2 * ML_HEADS, 0:1]
    capped = GATE_SOFTCAP * jnp.tanh(pre / GATE_SOFTCAP)
    is_i = lax.broadcasted_iota(jnp.int32, capped.shape, 0) < ML_HEADS
    m_gates = jnp.where(is_i, capped, -_softplus(-capped))
    g_go = [_gate_orientations(_lanes(g_gates, c), row, col) for c in range(NCB)]
    m_go = [_gate_orientations(_lanes(m_gates, c), row, col) for c in range(NCB)]


    heads = range(GDN_HEADS)
    probs = [(c, h) for c in range(NCB) for h in heads]
    q = [qkv_ref[h, c * CHUNK:(c + 1) * CHUNK, :] for c, h in probs]
    k = [qkv_ref[GDN_HEADS + h, c * CHUNK:(c + 1) * CHUNK, :] for c, h in probs]
    vb = [qkv_ref[2 * GDN_HEADS + h, c * CHUNK:(c + 1) * CHUNK, :].astype(BF16) for c, h in probs]
    kb = [ki.astype(BF16) for ki in k]
    kt = [ki.T for ki in k]
    gc_r = [g_go[c][0][h:h + 1, :] for c, h in probs]
    gc_c = [g_go[c][1][:, h:h + 1] for c, h in probs]
    beta_c = [g_go[c][2][:, GB0 + h:GB0 + h + 1] for c, h in probs]
    beta_r = [_lanes(beta_row, c)[h:h + 1, :] for c, h in probs]
    decay = [jnp.exp(jnp.where(causal, a - b, NEG)) for a, b in zip(gc_c, gc_r)]
    kq = [_dot(jnp.concatenate([kbi, qi.astype(BF16)], axis=0), kti.astype(BF16))
          for kbi, qi, kti in zip(kb, q, kt)]
    m = [jnp.where(strict, x[:CHUNK] * d * b, 0.0) for x, d, b in zip(kq, decay, beta_c)]
    attn = [(x[CHUNK:] * d).astype(BF16) for x, d in zip(kq, decay)]

    mheads = range(ML_HEADS)
    mprobs = [(c, h) for c in range(NCB) for h in mheads]
    mq = [chunk_rows(mq_ref, c, h * ML_DQK, ML_DQK) * (ML_DQK ** -0.5) for c, h in mprobs]
    mk = [chunk_rows(mk_ref, c, h * ML_DQK, ML_DQK) for c, h in mprobs]
    mkt = [x.T for x in mk]
    mqb = [x.astype(BF16) for x in mq]
    mvb = [chunk_rows(mv_ref, c, h * ML_DV, ML_DV).astype(BF16) for c, h in mprobs]
    b_r = [m_go[c][0][ML_HEADS + h:ML_HEADS + h + 1, :] for c, h in mprobs]
    b_c = [m_go[c][1][:, ML_HEADS + h:ML_HEADS + h + 1] for c, h in mprobs]
    i_c = [m_go[c][2][:, h:h + 1] for c, h in mprobs]
    i_r = [_lanes(m_gates, c)[h:h + 1, :] for c, h in mprobs]
    b_end = [x[:, CHUNK - 1:CHUNK] for x in b_r]
    dmat = [jnp.where(causal, bc - br + ir, NEG) for bc, br, ir in zip(b_c, b_r, i_r)]
    m_intra = [jnp.max(d, axis=-1, keepdims=True) for d in dmat]
    mqk = [_dot(a, b.astype(BF16)) for a, b in zip(mqb, mkt)]

    ml_state = {
        "c": [c_ref[h] for h in mheads],
        "n": [n_ref[h:h + 1, :] for h in mheads],
        "m": [m_ref[h:h + 1, 0:1] for h in mheads],
    }

    def mlstm_chunk(c):
        c_st, n_st, m_st = ml_state["c"], ml_state["n"], ml_state["m"]
        ids = [c * ML_HEADS + h for h in mheads]
        inter_log = [b_c[i] + m_st[h] for h, i in zip(mheads, ids)]
        m_t = [jnp.maximum(a, m_intra[i]) for a, i in zip(inter_log, ids)]
        inter = [jnp.exp(a - b) for a, b in zip(inter_log, m_t)]
        p = [jnp.exp(dmat[i] - mt) * mqk[i] for mt, i in zip(m_t, ids)]
        num = [it * _dot(mqb[i], c_st[h].astype(BF16)) + _dot(pi.astype(BF16), mvb[i])
               for h, i, it, pi in zip(mheads, ids, inter, p)]
        den = [it * jnp.sum(mq[i] * n_st[h], axis=-1, keepdims=True) + jnp.sum(pi, axis=-1, keepdims=True)
               for h, i, it, pi in zip(mheads, ids, inter, p)]
        hout = [nu / jnp.maximum(jnp.abs(de), jnp.exp(-mt)) for nu, de, mt in zip(num, den, m_t)]
        m_new = [mt[CHUNK - 1:CHUNK, :] for mt in m_t]
        carry = [jnp.exp(b_end[i] + m_st[h] - mn) for h, i, mn in zip(mheads, ids, m_new)]
        ktw = [(mkt[i] * jnp.exp(b_end[i] - b_r[i] + i_r[i] - mn)).astype(BF16) for i, mn in zip(ids, m_new)]
        kw = [mk[i] * jnp.exp(b_end[i] - b_c[i] + i_c[i] - mn) for i, mn in zip(ids, m_new)]
        ml_state["c"] = [ca * c_st[h] + _dot(x, mvb[i]) for h, i, ca, x in zip(mheads, ids, carry, ktw)]
        ml_state["n"] = [ca * n_st[h] + jnp.sum(x, axis=0, keepdims=True) for h, ca, x in zip(mheads, carry, kw)]
        ml_state["m"] = m_new
        for h, ho in zip(mheads, hout):
            og = chunk_rows(mo_ref, c, h * ML_DV, ML_DV)
            y_ref[c * CHUNK:(c + 1) * CHUNK, GDN_V_W + h * ML_DV:GDN_V_W + (h + 1) * ML_DV] = (
                jax.nn.sigmoid(og) * ho).astype(y_ref.dtype)

    between = {2 + c: [functools.partial(mlstm_chunk, c)] for c in range(NCB)}
    t_inv = _unit_lower_inverse(m, row, col, between)
    for h in mheads:
        c_ref[h] = ml_state["c"][h]
        n_ref[h:h + 1, :] = ml_state["n"][h]
        m_ref[h:h + 1, :] = jnp.broadcast_to(ml_state["m"][h], (1, LANES))

    tbeta = [ti * b for ti, b in zip(t_inv, beta_r)]
    u = [_dot(x.astype(BF16), vi) for x, vi in zip(tbeta, vb)]
    w = [_dot((x * jnp.exp(g)).astype(BF16), kbi) for x, g, kbi in zip(tbeta, gc_r, kb)]
    gc_last = [g[:, CHUNK - 1:CHUNK] for g in gc_r]
    kt_dec = [(kti * jnp.exp(gl - g)).astype(BF16) for kti, gl, g in zip(kt, gc_last, gc_r)]
    g_end = [jnp.exp(gl) for gl in gc_last]
    wq_lhs = [jnp.concatenate([wi, qi * jnp.exp(g)], axis=0).astype(BF16) for wi, qi, g in zip(w, q, gc_c)]

    st = [s_ref[h] for h in heads]
    o = []
    for c in range(NCB):
        i0 = c * GDN_HEADS
        wq = [_dot(wq_lhs[i0 + h], st[h].astype(BF16)) for h in heads]
        vn = [(u[i0 + h] - wq[h][:CHUNK]).astype(BF16) for h in heads]
        o += [wq[h][CHUNK:] + _dot(attn[i0 + h], vn[h]) for h in heads]
        st = [g_end[i0 + h] * st[h] + _dot(kt_dec[i0 + h], vn[h]) for h in heads]
    for h in heads:
        s_ref[h] = st[h]

    nw = nw_ref[...]
    for (c, h), oi in zip(probs, o):
        oi = oi * lax.rsqrt(jnp.mean(oi * oi, axis=-1, keepdims=True) + NORM_EPS) * nw
        oi = oi * _silu(chunk_rows(z_ref, c, h * GDN_DV, GDN_DV))
        y_ref[c * CHUNK:(c + 1) * CHUNK, h * GDN_DV:(h + 1) * GDN_DV] = oi.astype(y_ref.dtype)


def _mixer(qkv, rest, gt, pcol, norm_w, *, batch, seq):
    nt = seq // TB
    z_w = GDN_V_W

    def rows(b, t):
        return b * nt + t

    return pl.pallas_call(
        _mixer_kernel,
        grid=(batch, nt),
        in_specs=[
            pl.BlockSpec((GDN_QKV_W // LANES, TB, LANES), lambda b, t: (0, rows(b, t), 0)),
            pl.BlockSpec((TB, GDN_V_W), lambda b, t: (rows(b, t), 0)),
            pl.BlockSpec((TB, ML_QK_W), lambda b, t: (rows(b, t), z_w // ML_QK_W)),
            pl.BlockSpec((TB, ML_QK_W), lambda b, t: (rows(b, t), z_w // ML_QK_W + 1)),
            pl.BlockSpec((TB, ML_V_W), lambda b, t: (rows(b, t), (z_w + 2 * ML_QK_W) // ML_V_W)),
            pl.BlockSpec((TB, ML_V_W), lambda b, t: (rows(b, t), (z_w + 2 * ML_QK_W) // ML_V_W + 1)),
            pl.BlockSpec((GATE_W, TB), lambda b, t: (0, rows(b, t))),
            pl.BlockSpec((GATE_W, LANES), lambda b, t: (0, 0)),
            pl.BlockSpec((1, GDN_DV), lambda b, t: (0, 0)),
        ],
        out_specs=pl.BlockSpec((TB, D_MIX), lambda b, t: (rows(b, t), 0)),
        out_shape=jax.ShapeDtypeStruct((batch * seq, D_MIX), BF16),
        scratch_shapes=[
            pltpu.VMEM((GDN_HEADS, GDN_DK, GDN_DV), F32),
            pltpu.VMEM((ML_HEADS, ML_DQK, ML_DV), F32),
            pltpu.VMEM((SUBLANES, ML_DQK), F32),
            pltpu.VMEM((SUBLANES, LANES), F32),
        ],
        compiler_params=pltpu.CompilerParams(
            dimension_semantics=("parallel", "arbitrary"), vmem_limit_bytes=VMEM_LIMIT),
        name="mixer",
    )(qkv, rest, rest, rest, rest, rest, gt, pcol, norm_w)


def _out_proj_kernel(x_ref, y_ref, w_ref, o_ref, wb_ref):
    @pl.when(pl.program_id(0) == 0)
    def _():
        wb_ref[...] = w_ref[...].astype(BF16)

    o_ref[...] = x_ref[...] + _dot(y_ref[...], wb_ref[...])


def _out_proj(x, y, w, *, layer, tm=512):
    m, d = x.shape
    return pl.pallas_call(
        _out_proj_kernel,
        grid=(m // tm,),
        in_specs=[
            pl.BlockSpec((tm, d), lambda i: (i, 0)),
            pl.BlockSpec((tm, D_MIX), lambda i: (i, 0)),
            pl.BlockSpec((None, D_MIX, d), lambda i: (layer, 0, 0), pipeline_mode=pl.Buffered(1)),
        ],
        out_specs=pl.BlockSpec((tm, d), lambda i: (i, 0)),
        out_shape=jax.ShapeDtypeStruct((m, d), F32),
        scratch_shapes=[pltpu.VMEM((D_MIX, d), BF16)],
        compiler_params=pltpu.CompilerParams(
            dimension_semantics=("arbitrary",), vmem_limit_bytes=VMEM_LIMIT),
        name="out_proj",
    )(x, y, w)


def _pad_rows(v, offset):
    return jnp.zeros((GATE_W,), F32).at[offset:offset + v.shape[0]].set(v.astype(F32))


def kernel(x, ffn1_norm_w, ffn1_w_gate, ffn1_w_up, ffn1_w_down, mix_norm_w, w_in, conv_w, gdn_a_log, gdn_dt_bias, gdn_norm_w, ml_i_bias, ml_f_bias, w_out, ffn2_norm_w, ffn2_w_gate, ffn2_w_up, ffn2_w_down, final_norm_w):
    batch, seq, d = x.shape
    depth = ffn1_norm_w.shape[0]
    h = x.reshape(batch * seq, d)
    fw = final_norm_w.reshape(1, d).astype(F32)
    m_lo = GDN_W + 2 * GDN_HEADS
    for l in range(depth):
        d_in = w_in.shape[2]
        w_proj = jnp.pad(w_in[l], ((0, 0), (0, -d_in % LANES))).astype(BF16)
        gates = jnp.concatenate([w_proj[:, GDN_W:m_lo], w_proj[:, m_lo + ML_W:d_in]], axis=1)
        wgt = jnp.zeros((GATE_W, d), BF16).at[:N_GATE].set(gates.T)
        bias = _pad_rows(gdn_dt_bias[l], GA0) + _pad_rows(ml_i_bias[l], MI0) + _pad_rows(ml_f_bias[l], MF0)
        alog = _pad_rows(gdn_a_log[l], GA0)
        pcol = jnp.zeros((GATE_W, LANES), F32).at[:, 0].set(bias).at[:, 1].set(alog)

        h = _ffn(h, ffn1_norm_w[l].reshape(1, d), ffn1_w_gate, ffn1_w_up, ffn1_w_down, fw,
                 layer=l, final_norm=False)
        qkv, rest, gt = _in_proj(h, mix_norm_w[l].reshape(1, d), w_proj, wgt, conv_w[l].astype(F32), seq=seq)
        y = _mixer(qkv, rest, gt, pcol, gdn_norm_w[l].reshape(1, GDN_DV).astype(F32), batch=batch, seq=seq)
        h = _out_proj(h, y, w_out, layer=l)
        h = _ffn(h, ffn2_norm_w[l].reshape(1, d), ffn2_w_gate, ffn2_w_up, ffn2_w_down, fw,
                 layer=l, final_norm=(l == depth - 1))
    return h.reshape(batch, seq, d)
```
